```python
import jax, jax.numpy as jnp
from jax import lax
import numpy as np

D_MODEL = 1024
BATCH = 4
SEQ = 8192
DEPTH = 4

N_MIXERS = 3
RMS_EPS = 1e-6
LN_EPS = 1e-5
CONV_WIDTH = 3
SGU_CHUNK = 128
SGU_HALF = 2 * D_MODEL
SGU_GROUPS = 8
SGU_GROUP_DIM = SGU_HALF // SGU_GROUPS
RET_HEADS = 4
RET_QK_DIM = D_MODEL // RET_HEADS
RET_V_DIM = 2 * D_MODEL // RET_HEADS
RET_CHUNK = 128
ROPE_BASE = 10000.0
D_FF = -(-8 * D_MODEL // (3 * 256)) * 256
N_CONV = (DEPTH + 2) // 3
N_SGU = (DEPTH + 1) // 3
N_RET = DEPTH // 3

kernel_name = "hybrid_conv_sgu_retention_trunk"


def rmsnorm(x, g):
    xf = x.astype(jnp.float32)
    y = xf * lax.rsqrt(jnp.mean(xf * xf, axis=-1, keepdims=True) + RMS_EPS)
    return (y * g.astype(jnp.float32)).astype(x.dtype)


def layernorm(x, g, b):
    xf = x.astype(jnp.float32)
    mu = jnp.mean(xf, axis=-1, keepdims=True)
    xc = xf - mu
    y = xc * lax.rsqrt(jnp.mean(xc * xc, axis=-1, keepdims=True) + LN_EPS)
    return (y * g.astype(jnp.float32) + b.astype(jnp.float32)).astype(x.dtype)


def short_conv_mixer(h, w_in, conv_w, w_out):
    bch = h @ w_in
    b_gate, c_gate, z = jnp.split(bch, 3, axis=-1)
    cz = c_gate * z
    conv = lax.conv_general_dilated(
        cz, conv_w[:, None, :].astype(cz.dtype), window_strides=(1,),
        padding=[(CONV_WIDTH - 1, 0)], dimension_numbers=("NWC", "WIO", "NWC"),
        feature_group_count=D_MODEL)
    return (b_gate * conv) @ w_out


def chunked_sgu_mixer(h, w_in, ln_g, ln_b, w_s, b_s, w_out):
    bsz, seq = h.shape[0], h.shape[1]
    n_chunks = seq // SGU_CHUNK
    zz = jax.nn.gelu(h @ w_in, approximate=False)
    u, v = jnp.split(zz, 2, axis=-1)
    v = layernorm(v, ln_g, ln_b)
    v = v.reshape(bsz, n_chunks, SGU_CHUNK, SGU_GROUPS, SGU_GROUP_DIM)
    causal = jnp.tril(jnp.ones((SGU_CHUNK, SGU_CHUNK), dtype=bool))
    w = jnp.where(causal[None], w_s, jnp.zeros((), w_s.dtype))
    s = jnp.einsum("gts,bcsgd->bctgd", w, v) + b_s.T[None, None, :, :, None]
    s = s.reshape(bsz, seq, SGU_HALF)
    return (u * s) @ w_out


def rotary(x, cos, sin):
    half = x.shape[-1] // 2
    x1, x2 = x[..., :half], x[..., half:]
    return jnp.concatenate([x1 * cos - x2 * sin, x2 * cos + x1 * sin], axis=-1)


def retention_mixer(h, w_in, w_out):
    bsz, seq = h.shape[0], h.shape[1]
    n_chunks = seq // RET_CHUNK
    proj = h @ w_in
    q, k, v, g = jnp.split(proj, [D_MODEL, 2 * D_MODEL, 4 * D_MODEL], axis=-1)
    q = q.astype(jnp.float32).reshape(bsz, seq, RET_HEADS, RET_QK_DIM).transpose(0, 2, 1, 3)
    k = k.astype(jnp.float32).reshape(bsz, seq, RET_HEADS, RET_QK_DIM).transpose(0, 2, 1, 3)
    v = v.astype(jnp.float32).reshape(bsz, seq, RET_HEADS, RET_V_DIM).transpose(0, 2, 1, 3)
    half = RET_QK_DIM // 2
    pos = jnp.arange(seq, dtype=jnp.float32)
    inv_freq = ROPE_BASE ** (-(jnp.arange(half, dtype=jnp.float32) / half))
    ang = pos[:, None] * inv_freq[None, :]
    cos, sin = jnp.cos(ang), jnp.sin(ang)
    q = rotary(q, cos, sin)
    k = rotary(k, cos, sin) * (RET_QK_DIM ** -0.5)
    log_gamma = jnp.log(1.0 - 2.0 ** (-5.0 - jnp.arange(RET_HEADS, dtype=jnp.float32)))
    idx = jnp.arange(RET_CHUNK, dtype=jnp.float32)
    diff = idx[:, None] - idx[None, :]
    d_inner = jnp.where(diff[None] >= 0,
                        jnp.exp(jnp.maximum(diff, 0.0)[None] * log_gamma[:, None, None]), 0.0)
    k_decay = jnp.exp((RET_CHUNK - 1 - idx)[None, :] * log_gamma[:, None])
    q_decay = jnp.exp((idx + 1.0)[None, :] * log_gamma[:, None])
    chunk_decay = jnp.exp(RET_CHUNK * log_gamma)
    qc = q.reshape(bsz, RET_HEADS, n_chunks, RET_CHUNK, RET_QK_DIM)
    kc = k.reshape(bsz, RET_HEADS, n_chunks, RET_CHUNK, RET_QK_DIM)
    vc = v.reshape(bsz, RET_HEADS, n_chunks, RET_CHUNK, RET_V_DIM)
    scores = jnp.einsum("bhcjd,bhcmd->bhcjm", qc, kc) * d_inner[None, :, None]
    inner = jnp.einsum("bhcjm,bhcme->bhcje", scores, vc)

    def step(state, inp):
        q_i, k_i, v_i = inp
        cross = jnp.einsum("bhjd,bhde->bhje", q_i, state) * q_decay[None, :, :, None]
        new_state = state * chunk_decay[None, :, None, None] + jnp.einsum(
            "bhmd,bhme->bhde", k_i * k_decay[None, :, :, None], v_i)
        return new_state, cross

    state0 = jnp.zeros((bsz, RET_HEADS, RET_QK_DIM, RET_V_DIM), jnp.float32)
    _, cross = lax.scan(step, state0, (jnp.moveaxis(qc, 2, 0), jnp.moveaxis(kc, 2, 0), jnp.moveaxis(vc, 2, 0)))
    ret = inner + jnp.moveaxis(cross, 0, 2)
    o = ret.reshape(bsz, RET_HEADS, seq, RET_V_DIM).transpose(0, 2, 1, 3)
    o = o * lax.rsqrt(jnp.mean(o * o, axis=-1, keepdims=True) + RMS_EPS)
    o = o.reshape(bsz, seq, RET_HEADS * RET_V_DIM)
    y = jax.nn.silu(g.astype(jnp.float32)) * o
    return y.astype(h.dtype) @ w_out


def swiglu_ffn(h, w_gate, w_up, w_down):
    return (jax.nn.silu(h @ w_gate) * (h @ w_up)) @ w_down


def setup_inputs(seed: int = 0) -> dict:
    key = jax.random.key(seed)
    ks = jax.random.split(key, 20)
    f32 = jnp.float32
    nrm = lambda k, shape, scale: jax.random.normal(k, shape, f32) * scale
    D = D_MODEL
    return {
        "x": nrm(ks[0], (BATCH, SEQ, D), 1.0),
        "norm_mix_g": 1.0 + nrm(ks[1], (DEPTH, D), 0.01),
        "norm_ffn_g": 1.0 + nrm(ks[2], (DEPTH, D), 0.01),
        "final_norm_g": 1.0 + nrm(ks[3], (D,), 0.01),
        "conv_w_in": nrm(ks[4], (N_CONV, D, 3 * D), D ** -0.5),
        "conv_w": nrm(ks[5], (N_CONV, CONV_WIDTH, D), CONV_WIDTH ** -0.5),
        "conv_w_out": nrm(ks[6], (N_CONV, D, D), D ** -0.5),
        "sgu_w_in": nrm(ks[7], (N_SGU, D, 2 * SGU_HALF), D ** -0.5),
        "sgu_ln_g": 1.0 + nrm(ks[8], (N_SGU, SGU_HALF), 0.01),
        "sgu_ln_b": nrm(ks[9], (N_SGU, SGU_HALF), 0.01),
        "sgu_w_s": nrm(ks[10], (N_SGU, SGU_GROUPS, SGU_CHUNK, SGU_CHUNK), SGU_CHUNK ** -0.5),
        "sgu_b_s": 1.0 + nrm(ks[11], (N_SGU, SGU_GROUPS, SGU_CHUNK), 0.1),
        "sgu_w_out": nrm(ks[12], (N_SGU, SGU_HALF, D), SGU_HALF ** -0.5),
        "ret_w_in": nrm(ks[13], (N_RET, D, 6 * D), D ** -0.5),
        "ret_w_out": nrm(ks[14], (N_RET, 2 * D, D), (2 * D) ** -0.5),
        "ffn_w_gate": nrm(ks[15], (DEPTH, D, D_FF), D ** -0.5),
        "ffn_w_up": nrm(ks[16], (DEPTH, D, D_FF), D ** -0.5),
        "ffn_w_down": nrm(ks[17], (DEPTH, D_FF, D), D_FF ** -0.5),
    }


def reference(x, norm_mix_g, norm_ffn_g, final_norm_g, conv_w_in, conv_w, conv_w_out,
              sgu_w_in, sgu_ln_g, sgu_ln_b, sgu_w_s, sgu_b_s, sgu_w_out,
              ret_w_in, ret_w_out, ffn_w_gate, ffn_w_up, ffn_w_down):
    for i in range(DEPTH):
        kind = i % N_MIXERS
        j = i // N_MIXERS
        h = rmsnorm(x, norm_mix_g[i])
        if kind == 0:
            m = short_conv_mixer(h, conv_w_in[j], conv_w[j], conv_w_out[j])
        elif kind == 1:
            m = chunked_sgu_mixer(h, sgu_w_in[j], sgu_ln_g[j], sgu_ln_b[j], sgu_w_s[j], sgu_b_s[j], sgu_w_out[j])
        else:
            m = retention_mixer(h, ret_w_in[j], ret_w_out[j])
        x = x + m
        h = rmsnorm(x, norm_ffn_g[i])
        x = x + swiglu_ffn(h, ffn_w_gate[i], ffn_w_up[i], ffn_w_down[i])
    return rmsnorm(x, final_norm_g)
```

```python
import functools
import math

import jax
import jax.numpy as jnp
from jax import lax
from jax.experimental import pallas as pl
from jax.experimental.pallas import tpu as pltpu

F32 = jnp.float32
BF16 = jnp.bfloat16

D_MODEL = 1024
N_MIXERS = 3
RMS_EPS = 1e-6
LN_EPS = 1e-5
CONV_WIDTH = 3
SGU_CHUNK = 128
SGU_HALF = 2 * D_MODEL
SGU_GROUPS = 8
SGU_GROUP_DIM = SGU_HALF // SGU_GROUPS
RET_HEADS = 4
RET_QK_DIM = D_MODEL // RET_HEADS
RET_V_DIM = 2 * D_MODEL // RET_HEADS
RET_CHUNK = 128
ROPE_BASE = 10000.0

SUBLANES = 8
VMEM_LIMIT_BYTES = 56 * 1024 * 1024

TOKEN_TILE = 512
FFN_CHUNK = 256


def _rmsnorm(x, g):
    ms = jnp.mean(x * x, axis=-1, keepdims=True)
    return x * lax.rsqrt(ms + RMS_EPS) * g


def _dot(a, b):
    return jnp.dot(a, b, preferred_element_type=F32)


def _resident(shape):
    zeros = (0,) * len(shape)
    return pl.BlockSpec(shape, lambda *_: zeros, pipeline_mode=pl.Buffered(1))


def _params(n_axes):
    return pltpu.CompilerParams(
        dimension_semantics=("arbitrary",) * n_axes,
        vmem_limit_bytes=VMEM_LIMIT_BYTES)


def _ffn_kernel(x_ref, g_ref, wg_ref, wu_ref, wd_ref, fg_ref, o_ref, *, final):
    x = x_ref[...]
    h = _rmsnorm(x, g_ref[...]).astype(BF16)
    d_ff = wg_ref.shape[1]
    acc = x
    for f0 in range(0, d_ff, FFN_CHUNK):
        gate = _dot(h, wg_ref[:, f0:f0 + FFN_CHUNK])
        up = _dot(h, wu_ref[:, f0:f0 + FFN_CHUNK])
        act = (jax.nn.silu(gate) * up).astype(BF16)
        acc = acc + _dot(act, wd_ref[f0:f0 + FFN_CHUNK, :])
    if final:
        acc = _rmsnorm(acc, fg_ref[...])
    o_ref[...] = acc


def _ffn(x, g, wg, wu, wd, fg, *, final):
    n, d = x.shape
    d_ff = wg.shape[1]
    assert n % TOKEN_TILE == 0 and d_ff % FFN_CHUNK == 0
    tile = pl.BlockSpec((TOKEN_TILE, d), lambda i: (i, 0))
    return pl.pallas_call(
        functools.partial(_ffn_kernel, final=final),
        grid=(n // TOKEN_TILE,),
        in_specs=[tile, _resident((1, d)), _resident((d, d_ff)),
                  _resident((d, d_ff)), _resident((d_ff, d)), _resident((1, d))],
        out_specs=tile,
        out_shape=jax.ShapeDtypeStruct((n, d), F32),
        compiler_params=_params(1),
        name="ffn_final" if final else "ffn",
    )(x, g, wg, wu, wd, fg)


def _conv_kernel(x_ref, g_ref, win_ref, cw_ref, wout_ref, o_ref, cz_ref):
    tm = x_ref.shape[1]
    d = x_ref.shape[2]

    @pl.when(pl.program_id(1) == 0)
    def _():
        cz_ref[0:SUBLANES, :] = jnp.zeros((SUBLANES, d), F32)

    x = x_ref[0]
    h = _rmsnorm(x, g_ref[...]).astype(BF16)
    c_gate = _dot(h, win_ref[:, d:2 * d])
    z = _dot(h, win_ref[:, 2 * d:3 * d])
    cz = c_gate * z
    cz_ref[SUBLANES:SUBLANES + tm, :] = cz
    conv = cz * cw_ref[CONV_WIDTH - 1:CONV_WIDTH, :]
    for k in range(CONV_WIDTH - 1):
        shift = CONV_WIDTH - 1 - k
        conv = conv + cz_ref[SUBLANES - shift:SUBLANES - shift + tm, :] * cw_ref[k:k + 1, :]
    cz_ref[0:SUBLANES, :] = cz_ref[tm:tm + SUBLANES, :]
    b_gate = _dot(h, win_ref[:, 0:d])
    y = (b_gate * conv).astype(BF16)
    o_ref[0] = x + _dot(y, wout_ref[...])


def _conv_mixer(x, g, w_in, conv_w, w_out):
    b, s, d = x.shape
    assert s % TOKEN_TILE == 0
    tile = pl.BlockSpec((1, TOKEN_TILE, d), lambda i, j: (i, j, 0))
    return pl.pallas_call(
        _conv_kernel,
        grid=(b, s // TOKEN_TILE),
        in_specs=[tile, _resident((1, d)), _resident((d, 3 * d)),
                  _resident((CONV_WIDTH, d)), _resident((d, d))],
        out_specs=tile,
        out_shape=jax.ShapeDtypeStruct((b, s, d), F32),
        scratch_shapes=[pltpu.VMEM((SUBLANES + TOKEN_TILE, d), F32)],
        compiler_params=_params(2),
        name="conv_mixer",
    )(x, g, w_in, conv_w, w_out)


def _gelu(x):
    return 0.5 * x * (1.0 + lax.erf(x * math.sqrt(0.5)))


def _sgu_kernel(x_ref, g_ref, win_ref, lng_ref, lnb_ref, ws_ref, bs_ref, wout_ref,
                o_ref, v_ref, s_ref):
    tm = x_ref.shape[0]
    x = x_ref[...]
    h = _rmsnorm(x, g_ref[...]).astype(BF16)

    v = _gelu(_dot(h, win_ref[:, SGU_HALF:2 * SGU_HALF]))
    mu = jnp.mean(v, axis=-1, keepdims=True)
    vc = v - mu
    var = jnp.mean(vc * vc, axis=-1, keepdims=True)
    v_ref[...] = (vc * lax.rsqrt(var + LN_EPS) * lng_ref[...] + lnb_ref[...]).astype(BF16)

    row = lax.broadcasted_iota(jnp.int32, (SGU_CHUNK, SGU_CHUNK), 0)
    col = lax.broadcasted_iota(jnp.int32, (SGU_CHUNK, SGU_CHUNK), 1)
    causal = row >= col
    for grp in range(SGU_GROUPS):
        w = jnp.where(causal, ws_ref[grp], 0.0).astype(BF16)
        bias = bs_ref[grp]
        lanes = slice(grp * SGU_GROUP_DIM, (grp + 1) * SGU_GROUP_DIM)
        for t0 in range(0, tm, SGU_CHUNK):
            rows = slice(t0, t0 + SGU_CHUNK)
            s_ref[rows, lanes] = _dot(w, v_ref[rows, lanes]) + bias

    u = _gelu(_dot(h, win_ref[:, 0:SGU_HALF]))
    y = (u * s_ref[...]).astype(BF16)
    o_ref[...] = x + _dot(y, wout_ref[...])


def _sgu_mixer(x, g, w_in, ln_g, ln_b, w_s, b_s, w_out):
    n, d = x.shape
    assert n % TOKEN_TILE == 0 and TOKEN_TILE % SGU_CHUNK == 0
    tile = pl.BlockSpec((TOKEN_TILE, d), lambda i: (i, 0))
    return pl.pallas_call(
        _sgu_kernel,
        grid=(n // TOKEN_TILE,),
        in_specs=[tile, _resident((1, d)), _resident((d, 2 * SGU_HALF)),
                  _resident((1, SGU_HALF)), _resident((1, SGU_HALF)),
                  _resident((SGU_GROUPS, SGU_CHUNK, SGU_CHUNK)),
                  _resident((SGU_GROUPS, SGU_CHUNK, 1)),
                  _resident((SGU_HALF, d))],
        out_specs=tile,
        out_shape=jax.ShapeDtypeStruct((n, d), F32),
        scratch_shapes=[pltpu.VMEM((TOKEN_TILE, SGU_HALF), BF16),
                        pltpu.VMEM((TOKEN_TILE, SGU_HALF), F32)],
        compiler_params=_params(1),
        name="sgu_mixer",
    )(x, g, w_in, ln_g, ln_b, w_s, b_s, w_out)


def _rotary_halves(x1, x2, cos, sin):
    return x1 * cos - x2 * sin, x2 * cos + x1 * sin


def _ret_kernel(x_ref, g_ref, win_ref, wout_ref, cos_ref, sin_ref, o_ref,
                q_ref, k_ref, kd_ref, v_ref, y_ref, state_ref):
    tm = x_ref.shape[1]
    d = x_ref.shape[2]
    half = RET_QK_DIM // 2

    @pl.when(pl.program_id(1) == 0)
    def _():
        state_ref[...] = jnp.zeros(state_ref.shape, F32)

    x = x_ref[0]
    h = _rmsnorm(x, g_ref[...]).astype(BF16)
    cos = cos_ref[...]
    sin = sin_ref[...]

    pos = lax.rem(lax.broadcasted_iota(jnp.int32, (tm, 1), 0), RET_CHUNK).astype(F32)
    row = lax.broadcasted_iota(jnp.int32, (RET_CHUNK, RET_CHUNK), 0)
    col = lax.broadcasted_iota(jnp.int32, (RET_CHUNK, RET_CHUNK), 1)
    diff = (row - col).astype(F32)
    idx = lax.broadcasted_iota(jnp.int32, (RET_CHUNK, 1), 0).astype(F32)

    q = _dot(h, win_ref[:, 0:d])
    k = _dot(h, win_ref[:, d:2 * d])
    for hd in range(RET_HEADS):
        log_gamma = math.log(1.0 - 2.0 ** (-5.0 - hd))
        k_decay = jnp.exp((RET_CHUNK - 1.0 - pos) * log_gamma)
        lo = slice(hd * RET_QK_DIM, hd * RET_QK_DIM + half)
        hi = slice(hd * RET_QK_DIM + half, (hd + 1) * RET_QK_DIM)
        q1, q2 = _rotary_halves(q[:, lo], q[:, hi], cos, sin)
        q_ref[:, lo] = q1.astype(BF16)
        q_ref[:, hi] = q2.astype(BF16)
        k1, k2 = _rotary_halves(k[:, lo], k[:, hi], cos, sin)
        k1 = k1 * (RET_QK_DIM ** -0.5)
        k2 = k2 * (RET_QK_DIM ** -0.5)
        k_ref[:, lo] = k1.astype(BF16)
        k_ref[:, hi] = k2.astype(BF16)
        kd_ref[:, lo] = (k1 * k_decay).astype(BF16)
        kd_ref[:, hi] = (k2 * k_decay).astype(BF16)
    v_ref[...] = _dot(h, win_ref[:, 2 * d:4 * d]).astype(BF16)

    for hd in range(RET_HEADS):
        log_gamma = math.log(1.0 - 2.0 ** (-5.0 - hd))
        d_inner = jnp.where(diff >= 0, jnp.exp(jnp.maximum(diff, 0.0) * log_gamma), 0.0)
        q_decay = jnp.exp((idx + 1.0) * log_gamma)
        chunk_decay = math.exp(RET_CHUNK * log_gamma)
        qk_lanes = slice(hd * RET_QK_DIM, (hd + 1) * RET_QK_DIM)
        v_lanes = slice(hd * RET_V_DIM, (hd + 1) * RET_V_DIM)
        for t0 in range(0, tm, RET_CHUNK):
            rows = slice(t0, t0 + RET_CHUNK)
            qc = q_ref[rows, qk_lanes]
            vc = v_ref[rows, v_lanes]
            scores = lax.dot_general(qc, k_ref[rows, qk_lanes], (((1,), (1,)), ((), ())),
                                     preferred_element_type=F32) * d_inner
            inner = _dot(scores.astype(BF16), vc)
            state = state_ref[hd]
            cross = _dot(qc, state.astype(BF16)) * q_decay
            state_ref[hd] = state * chunk_decay + lax.dot_general(
                kd_ref[rows, qk_lanes], vc, (((0,), (0,)), ((), ())),
                preferred_element_type=F32)
            o = inner + cross
            o = o * lax.rsqrt(jnp.mean(o * o, axis=-1, keepdims=True) + RMS_EPS)
            y_ref[rows, v_lanes] = o

    gate = _dot(h, win_ref[:, 4 * d:6 * d])
    y = (jax.nn.silu(gate) * y_ref[...]).astype(BF16)
    o_ref[0] = x + _dot(y, wout_ref[...])


def _ret_mixer(x, g, w_in, w_out, cos, sin):
    b, s, d = x.shape
    assert s % TOKEN_TILE == 0 and TOKEN_TILE % RET_CHUNK == 0
    tile = pl.BlockSpec((1, TOKEN_TILE, d), lambda i, j: (i, j, 0))
    rope = pl.BlockSpec((TOKEN_TILE, RET_QK_DIM // 2), lambda i, j: (j, 0))
    return pl.pallas_call(
        _ret_kernel,
        grid=(b, s // TOKEN_TILE),
        in_specs=[tile, _resident((1, d)), _resident((d, 6 * d)), _resident((2 * d, d)),
                  rope, rope],
        out_specs=tile,
        out_shape=jax.ShapeDtypeStruct((b, s, d), F32),
        scratch_shapes=[pltpu.VMEM((TOKEN_TILE, d), BF16),
                        pltpu.VMEM((TOKEN_TILE, d), BF16),
                        pltpu.VMEM((TOKEN_TILE, d), BF16),
                        pltpu.VMEM((TOKEN_TILE, 2 * d), BF16),
                        pltpu.VMEM((TOKEN_TILE, 2 * d), F32),
                        pltpu.VMEM((RET_HEADS, RET_QK_DIM, RET_V_DIM), F32)],
        compiler_params=_params(2),
        name="ret_mixer",
    )(x, g, w_in, w_out, cos, sin)


def _rope_tables(seq):
    half = RET_QK_DIM // 2
    pos = jnp.arange(seq, dtype=F32)
    inv_freq = ROPE_BASE ** (-(jnp.arange(half, dtype=F32) / half))
    ang = pos[:, None] * inv_freq[None, :]
    return jnp.cos(ang), jnp.sin(ang)


def kernel(x, norm_mix_g, norm_ffn_g, final_norm_g, conv_w_in, conv_w, conv_w_out,
           sgu_w_in, sgu_ln_g, sgu_ln_b, sgu_w_s, sgu_b_s, sgu_w_out,
           ret_w_in, ret_w_out, ffn_w_gate, ffn_w_up, ffn_w_down):
    bsz, seq, d = x.shape
    depth = norm_mix_g.shape[0]
    cos, sin = _rope_tables(seq)
    row = lambda v: v.reshape(1, -1)
    for i in range(depth):
        kind = i % N_MIXERS
        j = i // N_MIXERS
        g = row(norm_mix_g[i])
        if kind == 0:
            x = _conv_mixer(x, g, conv_w_in[j].astype(BF16), conv_w[j],
                            conv_w_out[j].astype(BF16))
        elif kind == 1:
            x = _sgu_mixer(x.reshape(bsz * seq, d), g, sgu_w_in[j].astype(BF16),
                           row(sgu_ln_g[j]), row(sgu_ln_b[j]), sgu_w_s[j],
                           sgu_b_s[j][:, :, None], sgu_w_out[j].astype(BF16)
                           ).reshape(bsz, seq, d)
        else:
            x = _ret_mixer(x, g, ret_w_in[j].astype(BF16), ret_w_out[j].astype(BF16),
                           cos, sin)
        final = i == depth - 1
        x = _ffn(x.reshape(bsz * seq, d), row(norm_ffn_g[i]), ffn_w_gate[i].astype(BF16),
                 ffn_w_up[i].astype(BF16), ffn_w_down[i].astype(BF16),
                 row(final_norm_g), final=final).reshape(bsz, seq, d)
    return x
```

```python
import functools
import math

import jax
import jax.numpy as jnp
from jax import lax
from jax.experimental import pallas as pl
from jax.experimental.pallas import tpu as pltpu

F32 = jnp.float32
BF16 = jnp.bfloat16

D_MODEL = 1024
N_MIXERS = 3
RMS_EPS = 1e-6
LN_EPS = 1e-5
CONV_WIDTH = 3
SGU_CHUNK = 128
SGU_HALF = 2 * D_MODEL
SGU_GROUPS = 8
SGU_GROUP_DIM = SGU_HALF // SGU_GROUPS
RET_HEADS = 4
RET_QK_DIM = D_MODEL // RET_HEADS
RET_V_DIM = 2 * D_MODEL // RET_HEADS
ROPE_BASE = 10000.0
RET_BLOCK = 256

SUBLANES = 8
VMEM_LIMIT_BYTES = 56 * 1024 * 1024

FFN_TILE = 1024
CONV_TILE = 1024
SGU_TILE = 512
RET_TILE = 512
FFN_CHUNK = 256


def _rmsnorm(x, g):
    ms = jnp.mean(x * x, axis=-1, keepdims=True)
    return x * lax.rsqrt(ms + RMS_EPS) * g


def _dot(a, b):
    return jnp.dot(a, b, preferred_element_type=F32)


def _resident(shape):
    zeros = (0,) * len(shape)
    return pl.BlockSpec(shape, lambda *_: zeros, pipeline_mode=pl.Buffered(1))


def _resident_layer(layer, shape):
    index = (layer,) + (0,) * len(shape)
    return pl.BlockSpec((None,) + tuple(shape), lambda *_: index,
                        pipeline_mode=pl.Buffered(1))


def _params(n_axes):
    return pltpu.CompilerParams(
        dimension_semantics=("arbitrary",) * n_axes,
        vmem_limit_bytes=VMEM_LIMIT_BYTES)


def _ffn_kernel(x_ref, g_ref, wg_ref, wu_ref, wd_ref, fg_ref, o_ref, *, final):
    x = x_ref[...]
    h = _rmsnorm(x, g_ref[...]).astype(BF16)
    d_ff = wg_ref.shape[1]
    acc = x
    for f0 in range(0, d_ff, FFN_CHUNK):
        gate = _dot(h, wg_ref[:, f0:f0 + FFN_CHUNK])
        up = _dot(h, wu_ref[:, f0:f0 + FFN_CHUNK])
        act = (jax.nn.silu(gate) * up).astype(BF16)
        acc = acc + _dot(act, wd_ref[f0:f0 + FFN_CHUNK, :])
    if final:
        acc = _rmsnorm(acc, fg_ref[...])
    o_ref[...] = acc


def _ffn(x, g, wg, wu, wd, fg, layer, *, final):
    n, d = x.shape
    d_ff = wg.shape[2]
    assert n % FFN_TILE == 0 and d_ff % FFN_CHUNK == 0
    tile = pl.BlockSpec((FFN_TILE, d), lambda i: (i, 0))
    return pl.pallas_call(
        functools.partial(_ffn_kernel, final=final),
        grid=(n // FFN_TILE,),
        in_specs=[tile, _resident_layer(layer, (1, d)),
                  _resident_layer(layer, (d, d_ff)), _resident_layer(layer, (d, d_ff)),
                  _resident_layer(layer, (d_ff, d)), _resident((1, d))],
        out_specs=tile,
        out_shape=jax.ShapeDtypeStruct((n, d), F32),
        compiler_params=_params(1),
        name="ffn_final" if final else "ffn",
    )(x, g, wg, wu, wd, fg)


def _conv_kernel(x_ref, g_ref, win_ref, cw_ref, wout_ref, o_ref, cz_ref):
    tm = x_ref.shape[1]
    d = x_ref.shape[2]

    @pl.when(pl.program_id(1) == 0)
    def _():
        cz_ref[0:SUBLANES, :] = jnp.zeros((SUBLANES, d), F32)

    x = x_ref[0]
    h = _rmsnorm(x, g_ref[...]).astype(BF16)
    c_gate = _dot(h, win_ref[:, d:2 * d])
    z = _dot(h, win_ref[:, 2 * d:3 * d])
    cz = c_gate * z
    cz_ref[SUBLANES:SUBLANES + tm, :] = cz
    conv = cz * cw_ref[CONV_WIDTH - 1:CONV_WIDTH, :]
    for k in range(CONV_WIDTH - 1):
        shift = CONV_WIDTH - 1 - k
        conv = conv + cz_ref[SUBLANES - shift:SUBLANES - shift + tm, :] * cw_ref[k:k + 1, :]
    cz_ref[0:SUBLANES, :] = cz_ref[tm:tm + SUBLANES, :]
    b_gate = _dot(h, win_ref[:, 0:d])
    y = (b_gate * conv).astype(BF16)
    o_ref[0] = x + _dot(y, wout_ref[...])


def _conv_mixer(x, g, w_in, conv_w, w_out, layer, j):
    b, s, d = x.shape
    assert s % CONV_TILE == 0
    tile = pl.BlockSpec((1, CONV_TILE, d), lambda i, t: (i, t, 0))
    return pl.pallas_call(
        _conv_kernel,
        grid=(b, s // CONV_TILE),
        in_specs=[tile, _resident_layer(layer, (1, d)), _resident_layer(j, (d, 3 * d)),
                  _resident_layer(j, (CONV_WIDTH, d)), _resident_layer(j, (d, d))],
        out_specs=tile,
        out_shape=jax.ShapeDtypeStruct((b, s, d), F32),
        scratch_shapes=[pltpu.VMEM((SUBLANES + CONV_TILE, d), F32)],
        compiler_params=_params(2),
        name="conv_mixer",
    )(x, g, w_in, conv_w, w_out)


def _gelu(x):
    return 0.5 * x * (1.0 + lax.erf(x * math.sqrt(0.5)))


def _sgu_kernel(x_ref, g_ref, win_ref, lng_ref, lnb_ref, ws_ref, bs_ref, wout_ref,
                o_ref, v_ref, s_ref):
    tm = x_ref.shape[0]
    x = x_ref[...]
    h = _rmsnorm(x, g_ref[...]).astype(BF16)

    v = _gelu(_dot(h, win_ref[:, SGU_HALF:2 * SGU_HALF]))
    mu = jnp.mean(v, axis=-1, keepdims=True)
    vc = v - mu
    var = jnp.mean(vc * vc, axis=-1, keepdims=True)
    v_ref[...] = (vc * lax.rsqrt(var + LN_EPS) * lng_ref[...] + lnb_ref[...]).astype(BF16)

    row = lax.broadcasted_iota(jnp.int32, (SGU_CHUNK, SGU_CHUNK), 0)
    col = lax.broadcasted_iota(jnp.int32, (SGU_CHUNK, SGU_CHUNK), 1)
    causal = row >= col
    for grp in range(SGU_GROUPS):
        w = jnp.where(causal, ws_ref[grp], 0.0).astype(BF16)
        bias = bs_ref[grp]
        lanes = slice(grp * SGU_GROUP_DIM, (grp + 1) * SGU_GROUP_DIM)
        for t0 in range(0, tm, SGU_CHUNK):
            rows = slice(t0, t0 + SGU_CHUNK)
            s_ref[rows, lanes] = _dot(w, v_ref[rows, lanes]) + bias

    u = _gelu(_dot(h, win_ref[:, 0:SGU_HALF]))
    y = (u * s_ref[...]).astype(BF16)
    o_ref[...] = x + _dot(y, wout_ref[...])


def _sgu_mixer(x, g, w_in, ln_g, ln_b, w_s, b_s, w_out, layer, j):
    n, d = x.shape
    assert n % SGU_TILE == 0 and SGU_TILE % SGU_CHUNK == 0
    tile = pl.BlockSpec((SGU_TILE, d), lambda i: (i, 0))
    return pl.pallas_call(
        _sgu_kernel,
        grid=(n // SGU_TILE,),
        in_specs=[tile, _resident_layer(layer, (1, d)),
                  _resident_layer(j, (d, 2 * SGU_HALF)),
                  _resident_layer(j, (1, SGU_HALF)), _resident_layer(j, (1, SGU_HALF)),
                  _resident_layer(j, (SGU_GROUPS, SGU_CHUNK, SGU_CHUNK)),
                  _resident_layer(j, (SGU_GROUPS, SGU_CHUNK, 1)),
                  _resident_layer(j, (SGU_HALF, d))],
        out_specs=tile,
        out_shape=jax.ShapeDtypeStruct((n, d), F32),
        scratch_shapes=[pltpu.VMEM((SGU_TILE, SGU_HALF), BF16),
                        pltpu.VMEM((SGU_TILE, SGU_HALF), F32)],
        compiler_params=_params(1),
        name="sgu_mixer",
    )(x, g, w_in, ln_g, ln_b, w_s, b_s, w_out)


def _rotary_halves(x1, x2, cos, sin):
    return x1 * cos - x2 * sin, x2 * cos + x1 * sin


def _ret_kernel(x_ref, g_ref, win_ref, wout_ref, cos_ref, sin_ref, o_ref,
                q_ref, k_ref, kd_ref, v_ref, y_ref, state_ref):
    tm = x_ref.shape[1]
    d = x_ref.shape[2]
    half = RET_QK_DIM // 2

    @pl.when(pl.program_id(1) == 0)
    def _():
        state_ref[...] = jnp.zeros(state_ref.shape, F32)

    x = x_ref[0]
    h = _rmsnorm(x, g_ref[...]).astype(BF16)
    cos = cos_ref[...]
    sin = sin_ref[...]

    pos = lax.rem(lax.broadcasted_iota(jnp.int32, (tm, 1), 0), RET_BLOCK).astype(F32)
    row = lax.broadcasted_iota(jnp.int32, (RET_BLOCK, RET_BLOCK), 0)
    col = lax.broadcasted_iota(jnp.int32, (RET_BLOCK, RET_BLOCK), 1)
    diff = (row - col).astype(F32)
    idx = lax.broadcasted_iota(jnp.int32, (RET_BLOCK, 1), 0).astype(F32)

    q = _dot(h, win_ref[:, 0:d])
    k = _dot(h, win_ref[:, d:2 * d])
    for hd in range(RET_HEADS):
        log_gamma = math.log(1.0 - 2.0 ** (-5.0 - hd))
        k_decay = jnp.exp((RET_BLOCK - 1.0 - pos) * log_gamma)
        lo = slice(hd * RET_QK_DIM, hd * RET_QK_DIM + half)
        hi = slice(hd * RET_QK_DIM + half, (hd + 1) * RET_QK_DIM)
        q1, q2 = _rotary_halves(q[:, lo], q[:, hi], cos, sin)
        q_ref[:, lo] = q1.astype(BF16)
        q_ref[:, hi] = q2.astype(BF16)
        k1, k2 = _rotary_halves(k[:, lo], k[:, hi], cos, sin)
        k1 = k1 * (RET_QK_DIM ** -0.5)
        k2 = k2 * (RET_QK_DIM ** -0.5)
        k_ref[:, lo] = k1.astype(BF16)
        k_ref[:, hi] = k2.astype(BF16)
        kd_ref[:, lo] = (k1 * k_decay).astype(BF16)
        kd_ref[:, hi] = (k2 * k_decay).astype(BF16)
    v_ref[...] = _dot(h, win_ref[:, 2 * d:4 * d]).astype(BF16)

    for hd in range(RET_HEADS):
        log_gamma = math.log(1.0 - 2.0 ** (-5.0 - hd))
        d_inner = jnp.where(diff >= 0, jnp.exp(jnp.maximum(diff, 0.0) * log_gamma), 0.0)
        q_decay = jnp.exp((idx + 1.0) * log_gamma)
        block_decay = math.exp(RET_BLOCK * log_gamma)
        qk_lanes = slice(hd * RET_QK_DIM, (hd + 1) * RET_QK_DIM)
        v_lanes = slice(hd * RET_V_DIM, (hd + 1) * RET_V_DIM)
        for t0 in range(0, tm, RET_BLOCK):
            rows = slice(t0, t0 + RET_BLOCK)
            qc = q_ref[rows, qk_lanes]
            vc = v_ref[rows, v_lanes]
            scores = lax.dot_general(qc, k_ref[rows, qk_lanes], (((1,), (1,)), ((), ())),
                                     preferred_element_type=F32) * d_inner
            inner = _dot(scores.astype(BF16), vc)
            state = state_ref[hd]
            cross = _dot(qc, state.astype(BF16)) * q_decay
            state_ref[hd] = state * block_decay + lax.dot_general(
                kd_ref[rows, qk_lanes], vc, (((0,), (0,)), ((), ())),
                preferred_element_type=F32)
            o = inner + cross
            o = o * lax.rsqrt(jnp.mean(o * o, axis=-1, keepdims=True) + RMS_EPS)
            y_ref[rows, v_lanes] = o

    gate = _dot(h, win_ref[:, 4 * d:6 * d])
    y = (jax.nn.silu(gate) * y_ref[...]).astype(BF16)
    o_ref[0] = x + _dot(y, wout_ref[...])


def _ret_mixer(x, g, w_in, w_out, cos, sin, layer, j):
    b, s, d = x.shape
    assert s % RET_TILE == 0 and RET_TILE % RET_BLOCK == 0
    tile = pl.BlockSpec((1, RET_TILE, d), lambda i, t: (i, t, 0))
    rope = pl.BlockSpec((RET_TILE, RET_QK_DIM // 2), lambda i, t: (t, 0))
    return pl.pallas_call(
        _ret_kernel,
        grid=(b, s // RET_TILE),
        in_specs=[tile, _resident_layer(layer, (1, d)), _resident_layer(j, (d, 6 * d)),
                  _resident_layer(j, (2 * d, d)), rope, rope],
        out_specs=tile,
        out_shape=jax.ShapeDtypeStruct((b, s, d), F32),
        scratch_shapes=[pltpu.VMEM((RET_TILE, d), BF16),
                        pltpu.VMEM((RET_TILE, d), BF16),
                        pltpu.VMEM((RET_TILE, d), BF16),
                        pltpu.VMEM((RET_TILE, 2 * d), BF16),
                        pltpu.VMEM((RET_TILE, 2 * d), F32),
                        pltpu.VMEM((RET_HEADS, RET_QK_DIM, RET_V_DIM), F32)],
        compiler_params=_params(2),
        name="ret_mixer",
    )(x, g, w_in, w_out, cos, sin)


def _rope_tables(seq):
    half = RET_QK_DIM // 2
    pos = jnp.arange(seq, dtype=F32)
    inv_freq = ROPE_BASE ** (-(jnp.arange(half, dtype=F32) / half))
    ang = pos[:, None] * inv_freq[None, :]
    return jnp.cos(ang), jnp.sin(ang)


def kernel(x, norm_mix_g, norm_ffn_g, final_norm_g, conv_w_in, conv_w, conv_w_out,
           sgu_w_in, sgu_ln_g, sgu_ln_b, sgu_w_s, sgu_b_s, sgu_w_out,
           ret_w_in, ret_w_out, ffn_w_gate, ffn_w_up, ffn_w_down):
    bsz, seq, d = x.shape
    depth = norm_mix_g.shape[0]
    cos, sin = _rope_tables(seq)
    rows = lambda v: v[:, None, :]
    norm_mix_g, norm_ffn_g = rows(norm_mix_g), rows(norm_ffn_g)
    final_norm_g = final_norm_g.reshape(1, d)
    sgu_ln_g, sgu_ln_b = rows(sgu_ln_g), rows(sgu_ln_b)
    sgu_b_s = sgu_b_s[..., None]
    (conv_w_in, conv_w_out, sgu_w_in, sgu_w_out, ret_w_in, ret_w_out,
     ffn_w_gate, ffn_w_up, ffn_w_down) = (
         w.astype(BF16) for w in (conv_w_in, conv_w_out, sgu_w_in, sgu_w_out, ret_w_in,
                                  ret_w_out, ffn_w_gate, ffn_w_up, ffn_w_down))
    for i in range(depth):
        kind = i % N_MIXERS
        j = i // N_MIXERS
        if kind == 0:
            x = _conv_mixer(x, norm_mix_g, conv_w_in, conv_w, conv_w_out, i, j)
        elif kind == 1:
            x = _sgu_mixer(x.reshape(bsz * seq, d), norm_mix_g, sgu_w_in, sgu_ln_g, sgu_ln_b,
                           sgu_w_s, sgu_b_s, sgu_w_out, i, j).reshape(bsz, seq, d)
        else:
            x = _ret_mixer(x, norm_mix_g, ret_w_in, ret_w_out, cos, sin, i, j)
        x = _ffn(x.reshape(bsz * seq, d), norm_ffn_g, ffn_w_gate, ffn_w_up, ffn_w_down,
                 final_norm_g, i, final=i == depth - 1).reshape(bsz, seq, d)
    return x
```

```python
import functools
import math

import jax
import jax.numpy as jnp
from jax import lax
from jax.experimental import pallas as pl
from jax.experimental.pallas import tpu as pltpu

F32 = jnp.float32
BF16 = jnp.bfloat16

D_MODEL = 1024
N_MIXERS = 3
RMS_EPS = 1e-6
LN_EPS = 1e-5
CONV_WIDTH = 3
SGU_CHUNK = 128
SGU_HALF = 2 * D_MODEL
SGU_GROUPS = 8
SGU_GROUP_DIM = SGU_HALF // SGU_GROUPS
RET_HEADS = 4
RET_QK_DIM = D_MODEL // RET_HEADS
RET_V_DIM = 2 * D_MODEL // RET_HEADS
ROPE_BASE = 10000.0
RET_BLOCK = 256

SUBLANES = 8
BF16_SUBLANES = 16
VMEM_LIMIT_BYTES = 56 * 1024 * 1024

FFN_TILE = 1024
CONV_TILE = 1024
SGU_TILE = 512
RET_TILE = 512
FFN_CHUNK = 256


def _rmsnorm(x, g):
    ms = jnp.mean(x * x, axis=-1, keepdims=True)
    return x * lax.rsqrt(ms + RMS_EPS) * g


def _dot(a, b):
    return jnp.dot(a, b, preferred_element_type=F32)


def _resident(shape):
    zeros = (0,) * len(shape)
    return pl.BlockSpec(shape, lambda *_: zeros, pipeline_mode=pl.Buffered(1))


def _resident_layer(layer, shape):
    index = (layer,) + (0,) * len(shape)
    return pl.BlockSpec((None,) + tuple(shape), lambda *_: index,
                        pipeline_mode=pl.Buffered(1))


def _params(n_axes):
    return pltpu.CompilerParams(
        dimension_semantics=("arbitrary",) * n_axes,
        vmem_limit_bytes=VMEM_LIMIT_BYTES)


def _cast_block_rows(rows, steps):
    for r in range(BF16_SUBLANES, rows + 1, BF16_SUBLANES):
        if rows % r == 0 and rows // r <= steps:
            return r
    raise ValueError(f"no row block for {rows} rows in {steps} steps")


def _call(body, name, grid, in_specs, args, out_spec, out_shape, scratch_shapes, casts):
    steps = math.prod(grid)

    def step_number(*idx):
        n = idx[0]
        for k, extent in zip(idx[1:], grid[1:]):
            n = n * extent + k
        return n

    cast_in, cast_out, cast_shapes = [], [], []
    for w, layer in casts:
        _, rows, cols = w.shape
        block = _cast_block_rows(rows, steps)
        last = rows // block - 1
        cast_in.append(pl.BlockSpec(
            (None, block, cols),
            lambda *idx, layer=layer, last=last: (layer, jnp.minimum(step_number(*idx), last), 0)))
        cast_out.append(pl.BlockSpec(
            (block, cols), lambda *idx, last=last: (jnp.minimum(step_number(*idx), last), 0)))
        cast_shapes.append(jax.ShapeDtypeStruct((rows, cols), BF16))
    n_in, n_cast = len(args), len(casts)

    def kern(*refs):
        in_refs = refs[:n_in]
        src_refs = refs[n_in:n_in + n_cast]
        out_ref = refs[n_in + n_cast]
        dst_refs = refs[n_in + n_cast + 1:n_in + 2 * n_cast + 1]
        scratch = refs[n_in + 2 * n_cast + 1:]
        body(*in_refs, out_ref, *scratch)
        for src, dst in zip(src_refs, dst_refs):
            dst[...] = src[...].astype(BF16)

    outs = pl.pallas_call(
        kern,
        grid=grid,
        in_specs=list(in_specs) + cast_in,
        out_specs=[out_spec] + cast_out,
        out_shape=[out_shape] + cast_shapes,
        scratch_shapes=scratch_shapes,
        compiler_params=_params(len(grid)),
        name=name,
    )(*args, *(w for w, _ in casts))
    return outs[0], list(outs[1:])


def _ffn_kernel(x_ref, g_ref, wg_ref, wu_ref, wd_ref, fg_ref, o_ref, *, final):
    x = x_ref[...]
    h = _rmsnorm(x, g_ref[...]).astype(BF16)
    d_ff = wg_ref.shape[1]
    acc = x
    for f0 in range(0, d_ff, FFN_CHUNK):
        gate = _dot(h, wg_ref[:, f0:f0 + FFN_CHUNK])
        up = _dot(h, wu_ref[:, f0:f0 + FFN_CHUNK])
        act = (jax.nn.silu(gate) * up).astype(BF16)
        acc = acc + _dot(act, wd_ref[f0:f0 + FFN_CHUNK, :])
    if final:
        acc = _rmsnorm(acc, fg_ref[...])
    o_ref[...] = acc


def _ffn(x, g, weights, fg, layer, casts, *, final):
    wg, wu, wd = weights
    n, d = x.shape
    d_ff = wg.shape[1]
    assert n % FFN_TILE == 0 and d_ff % FFN_CHUNK == 0
    tile = pl.BlockSpec((FFN_TILE, d), lambda i: (i, 0))
    return _call(
        functools.partial(_ffn_kernel, final=final),
        "ffn_final" if final else "ffn",
        (n // FFN_TILE,),
        [tile, _resident_layer(layer, (1, d)), _resident((d, d_ff)), _resident((d, d_ff)),
         _resident((d_ff, d)), _resident((1, d))],
        (x, g, wg, wu, wd, fg),
        tile, jax.ShapeDtypeStruct((n, d), F32), [], casts)


def _conv_kernel(x_ref, g_ref, win_ref, cw_ref, wout_ref, o_ref, cz_ref):
    tm = x_ref.shape[1]
    d = x_ref.shape[2]

    @pl.when(pl.program_id(1) == 0)
    def _():
        cz_ref[0:SUBLANES, :] = jnp.zeros((SUBLANES, d), F32)

    x = x_ref[0]
    h = _rmsnorm(x, g_ref[...]).astype(BF16)
    c_gate = _dot(h, win_ref[:, d:2 * d])
    z = _dot(h, win_ref[:, 2 * d:3 * d])
    cz = c_gate * z
    cz_ref[SUBLANES:SUBLANES + tm, :] = cz
    conv = cz * cw_ref[CONV_WIDTH - 1:CONV_WIDTH, :]
    for k in range(CONV_WIDTH - 1):
        shift = CONV_WIDTH - 1 - k
        conv = conv + cz_ref[SUBLANES - shift:SUBLANES - shift + tm, :] * cw_ref[k:k + 1, :]
    cz_ref[0:SUBLANES, :] = cz_ref[tm:tm + SUBLANES, :]
    b_gate = _dot(h, win_ref[:, 0:d])
    y = (b_gate * conv).astype(BF16)
    o_ref[0] = x + _dot(y, wout_ref[...])


def _conv_mixer(x, g, weights, conv_w, layer, j, casts):
    w_in, w_out = weights
    b, s, d = x.shape
    assert s % CONV_TILE == 0
    tile = pl.BlockSpec((1, CONV_TILE, d), lambda i, t: (i, t, 0))
    return _call(
        _conv_kernel, "conv_mixer", (b, s // CONV_TILE),
        [tile, _resident_layer(layer, (1, d)), _resident((d, 3 * d)),
         _resident_layer(j, (CONV_WIDTH, d)), _resident((d, d))],
        (x, g, w_in, conv_w, w_out),
        tile, jax.ShapeDtypeStruct((b, s, d), F32),
        [pltpu.VMEM((SUBLANES + CONV_TILE, d), F32)], casts)


def _gelu(x):
    return 0.5 * x * (1.0 + lax.erf(x * math.sqrt(0.5)))


def _sgu_kernel(x_ref, g_ref, win_ref, lng_ref, lnb_ref, ws_ref, bs_ref, wout_ref,
                o_ref, v_ref, s_ref):
    tm = x_ref.shape[0]
    x = x_ref[...]
    h = _rmsnorm(x, g_ref[...]).astype(BF16)

    v = _gelu(_dot(h, win_ref[:, SGU_HALF:2 * SGU_HALF]))
    mu = jnp.mean(v, axis=-1, keepdims=True)
    vc = v - mu
    var = jnp.mean(vc * vc, axis=-1, keepdims=True)
    v_ref[...] = (vc * lax.rsqrt(var + LN_EPS) * lng_ref[...] + lnb_ref[...]).astype(BF16)

    row = lax.broadcasted_iota(jnp.int32, (SGU_CHUNK, SGU_CHUNK), 0)
    col = lax.broadcasted_iota(jnp.int32, (SGU_CHUNK, SGU_CHUNK), 1)
    causal = row >= col
    for grp in range(SGU_GROUPS):
        w = jnp.where(causal, ws_ref[grp], 0.0).astype(BF16)
        bias = bs_ref[grp]
        lanes = slice(grp * SGU_GROUP_DIM, (grp + 1) * SGU_GROUP_DIM)
        for t0 in range(0, tm, SGU_CHUNK):
            rows = slice(t0, t0 + SGU_CHUNK)
            s_ref[rows, lanes] = _dot(w, v_ref[rows, lanes]) + bias

    u = _gelu(_dot(h, win_ref[:, 0:SGU_HALF]))
    y = (u * s_ref[...]).astype(BF16)
    o_ref[...] = x + _dot(y, wout_ref[...])


def _sgu_mixer(x, g, weights, ln_g, ln_b, w_s, b_s, layer, j, casts):
    w_in, w_out = weights
    n, d = x.shape
    assert n % SGU_TILE == 0 and SGU_TILE % SGU_CHUNK == 0
    tile = pl.BlockSpec((SGU_TILE, d), lambda i: (i, 0))
    return _call(
        _sgu_kernel, "sgu_mixer", (n // SGU_TILE,),
        [tile, _resident_layer(layer, (1, d)), _resident((d, 2 * SGU_HALF)),
         _resident_layer(j, (1, SGU_HALF)), _resident_layer(j, (1, SGU_HALF)),
         _resident_layer(j, (SGU_GROUPS, SGU_CHUNK, SGU_CHUNK)),
         _resident_layer(j, (SGU_GROUPS, SGU_CHUNK, 1)), _resident((SGU_HALF, d))],
        (x, g, w_in, ln_g, ln_b, w_s, b_s, w_out),
        tile, jax.ShapeDtypeStruct((n, d), F32),
        [pltpu.VMEM((SGU_TILE, SGU_HALF), BF16), pltpu.VMEM((SGU_TILE, SGU_HALF), F32)],
        casts)


def _rotary_halves(x1, x2, cos, sin):
    return x1 * cos - x2 * sin, x2 * cos + x1 * sin


def _ret_kernel(x_ref, g_ref, win_ref, wout_ref, cos_ref, sin_ref, o_ref,
                q_ref, k_ref, kd_ref, v_ref, y_ref, state_ref):
    tm = x_ref.shape[1]
    d = x_ref.shape[2]
    half = RET_QK_DIM // 2

    @pl.when(pl.program_id(1) == 0)
    def _():
        state_ref[...] = jnp.zeros(state_ref.shape, F32)

    x = x_ref[0]
    h = _rmsnorm(x, g_ref[...]).astype(BF16)
    cos = cos_ref[...]
    sin = sin_ref[...]

    pos = lax.rem(lax.broadcasted_iota(jnp.int32, (tm, 1), 0), RET_BLOCK).astype(F32)
    row = lax.broadcasted_iota(jnp.int32, (RET_BLOCK, RET_BLOCK), 0)
    col = lax.broadcasted_iota(jnp.int32, (RET_BLOCK, RET_BLOCK), 1)
    diff = (row - col).astype(F32)
    idx = lax.broadcasted_iota(jnp.int32, (RET_BLOCK, 1), 0).astype(F32)

    q = _dot(h, win_ref[:, 0:d])
    k = _dot(h, win_ref[:, d:2 * d])
    for hd in range(RET_HEADS):
        log_gamma = math.log(1.0 - 2.0 ** (-5.0 - hd))
        k_decay = jnp.exp((RET_BLOCK - 1.0 - pos) * log_gamma)
        lo = slice(hd * RET_QK_DIM, hd * RET_QK_DIM + half)
        hi = slice(hd * RET_QK_DIM + half, (hd + 1) * RET_QK_DIM)
        q1, q2 = _rotary_halves(q[:, lo], q[:, hi], cos, sin)
        q_ref[:, lo] = q1.astype(BF16)
        q_ref[:, hi] = q2.astype(BF16)
        k1, k2 = _rotary_halves(k[:, lo], k[:, hi], cos, sin)
        k1 = k1 * (RET_QK_DIM ** -0.5)
        k2 = k2 * (RET_QK_DIM ** -0.5)
        k_ref[:, lo] = k1.astype(BF16)
        k_ref[:, hi] = k2.astype(BF16)
        kd_ref[:, lo] = (k1 * k_decay).astype(BF16)
        kd_ref[:, hi] = (k2 * k_decay).astype(BF16)
    v_ref[...] = _dot(h, win_ref[:, 2 * d:4 * d]).astype(BF16)

    for hd in range(RET_HEADS):
        log_gamma = math.log(1.0 - 2.0 ** (-5.0 - hd))
        d_inner = jnp.where(diff >= 0, jnp.exp(jnp.maximum(diff, 0.0) * log_gamma), 0.0)
        q_decay = jnp.exp((idx + 1.0) * log_gamma)
        block_decay = math.exp(RET_BLOCK * log_gamma)
        qk_lanes = slice(hd * RET_QK_DIM, (hd + 1) * RET_QK_DIM)
        v_lanes = slice(hd * RET_V_DIM, (hd + 1) * RET_V_DIM)
        for t0 in range(0, tm, RET_BLOCK):
            rows = slice(t0, t0 + RET_BLOCK)
            qc = q_ref[rows, qk_lanes]
            vc = v_ref[rows, v_lanes]
            scores = lax.dot_general(qc, k_ref[rows, qk_lanes], (((1,), (1,)), ((), ())),
                                     preferred_element_type=F32) * d_inner
            inner = _dot(scores.astype(BF16), vc)
            state = state_ref[hd]
            cross = _dot(qc, state.astype(BF16)) * q_decay
            state_ref[hd] = state * block_decay + lax.dot_general(
                kd_ref[rows, qk_lanes], vc, (((0,), (0,)), ((), ())),
                preferred_element_type=F32)
            o = inner + cross
            o = o * lax.rsqrt(jnp.mean(o * o, axis=-1, keepdims=True) + RMS_EPS)
            y_ref[rows, v_lanes] = o

    gate = _dot(h, win_ref[:, 4 * d:6 * d])
    y = (jax.nn.silu(gate) * y_ref[...]).astype(BF16)
    o_ref[0] = x + _dot(y, wout_ref[...])


def _ret_mixer(x, g, weights, cos, sin, layer, casts):
    w_in, w_out = weights
    b, s, d = x.shape
    assert s % RET_TILE == 0 and RET_TILE % RET_BLOCK == 0
    tile = pl.BlockSpec((1, RET_TILE, d), lambda i, t: (i, t, 0))
    rope = pl.BlockSpec((RET_TILE, RET_QK_DIM // 2), lambda i, t: (t, 0))
    return _call(
        _ret_kernel, "ret_mixer", (b, s // RET_TILE),
        [tile, _resident_layer(layer, (1, d)), _resident((d, 6 * d)), _resident((2 * d, d)),
         rope, rope],
        (x, g, w_in, w_out, cos, sin),
        tile, jax.ShapeDtypeStruct((b, s, d), F32),
        [pltpu.VMEM((RET_TILE, d), BF16), pltpu.VMEM((RET_TILE, d), BF16),
         pltpu.VMEM((RET_TILE, d), BF16), pltpu.VMEM((RET_TILE, 2 * d), BF16),
         pltpu.VMEM((RET_TILE, 2 * d), F32),
         pltpu.VMEM((RET_HEADS, RET_QK_DIM, RET_V_DIM), F32)],
        casts)


def _rope_tables(seq):
    half = RET_QK_DIM // 2
    pos = jnp.arange(seq, dtype=F32)
    inv_freq = ROPE_BASE ** (-(jnp.arange(half, dtype=F32) / half))
    ang = pos[:, None] * inv_freq[None, :]
    return jnp.cos(ang), jnp.sin(ang)


def kernel(x, norm_mix_g, norm_ffn_g, final_norm_g, conv_w_in, conv_w, conv_w_out,
           sgu_w_in, sgu_ln_g, sgu_ln_b, sgu_w_s, sgu_b_s, sgu_w_out,
           ret_w_in, ret_w_out, ffn_w_gate, ffn_w_up, ffn_w_down):
    bsz, seq, d = x.shape
    depth = norm_mix_g.shape[0]
    cos, sin = _rope_tables(seq)
    rows = lambda v: v[:, None, :]
    norm_mix_g, norm_ffn_g = rows(norm_mix_g), rows(norm_ffn_g)
    final_norm_g = final_norm_g.reshape(1, d)
    sgu_ln_g, sgu_ln_b = rows(sgu_ln_g), rows(sgu_ln_b)
    sgu_b_s = sgu_b_s[..., None]

    mixer_weights = {0: (conv_w_in, conv_w_out), 1: (sgu_w_in, sgu_w_out),
                     2: (ret_w_in, ret_w_out)}
    stages = []
    for i in range(depth):
        j = i // N_MIXERS
        stages.append([(w, j) for w in mixer_weights[i % N_MIXERS]])
        stages.append([(w, i) for w in (ffn_w_gate, ffn_w_up, ffn_w_down)])
    stages.append([])

    weights = [w[layer].astype(BF16) for w, layer in stages[0]]
    for i in range(depth):
        kind = i % N_MIXERS
        j = i // N_MIXERS
        casts = stages[2 * i + 1]
        if kind == 0:
            x, weights = _conv_mixer(x, norm_mix_g, weights, conv_w, i, j, casts)
        elif kind == 1:
            x, weights = _sgu_mixer(x.reshape(bsz * seq, d), norm_mix_g, weights, sgu_ln_g,
                                    sgu_ln_b, sgu_w_s, sgu_b_s, i, j, casts)
            x = x.reshape(bsz, seq, d)
        else:
            x, weights = _ret_mixer(x, norm_mix_g, weights, cos, sin, i, casts)
        x, weights = _ffn(x.reshape(bsz * seq, d), norm_ffn_g, weights, final_norm_g, i,
                          stages[2 * i + 2], final=i == depth - 1)
        x = x.reshape(bsz, seq, d)
    return x
```

```python
import functools
import math

import jax
import jax.numpy as jnp
from jax import lax
from jax.experimental import pallas as pl
from jax.experimental.pallas import tpu as pltpu

F32 = jnp.float32
BF16 = jnp.bfloat16

D_MODEL = 1024
N_MIXERS = 3
RMS_EPS = 1e-6
LN_EPS = 1e-5
CONV_WIDTH = 3
SGU_CHUNK = 128
SGU_HALF = 2 * D_MODEL
SGU_GROUPS = 8
SGU_GROUP_DIM = SGU_HALF // SGU_GROUPS
RET_HEADS = 4
RET_QK_DIM = D_MODEL // RET_HEADS
RET_V_DIM = 2 * D_MODEL // RET_HEADS
ROPE_BASE = 10000.0
RET_BLOCK = 256

SUBLANES = 8
LANES = 128
BF16_SUBLANES = 16
VMEM_LIMIT_BYTES = 56 * 1024 * 1024

FFN_TILE = 1024
CONV_TILE = 1024
SGU_TILE = 512
RET_TILE = 512
FFN_CHUNK = 256
FFN_NORM_AHEAD_BLOCKS = 4


def _rmsnorm(x, g):
    ms = jnp.mean(x * x, axis=-1, keepdims=True)
    return x * lax.rsqrt(ms + RMS_EPS) * g


def _dot(a, b):
    return jnp.dot(a, b, preferred_element_type=F32)


def _zero_row_after(value, width):
    bits = pltpu.bitcast(value, jnp.uint32)
    rows, cols = bits.shape
    tile = None
    for r0 in range(0, rows, SUBLANES):
        for c0 in range(0, cols, LANES):
            part = bits[r0:r0 + SUBLANES, c0:c0 + LANES]
            tile = part if tile is None else tile | part
    sixteen = jnp.uint32(16)
    tile = lax.shift_right_logical(lax.shift_right_logical(tile, sixteen), sixteen)
    row = pltpu.bitcast(tile, F32)[0:1, :]
    return jnp.concatenate([row] * (width // LANES), axis=1)


def _resident(shape):
    zeros = (0,) * len(shape)
    return pl.BlockSpec(shape, lambda *_: zeros, pipeline_mode=pl.Buffered(1))


def _resident_layer(layer, shape):
    index = (layer,) + (0,) * len(shape)
    return pl.BlockSpec((None,) + tuple(shape), lambda *_: index,
                        pipeline_mode=pl.Buffered(1))


def _params(n_axes):
    return pltpu.CompilerParams(
        dimension_semantics=("arbitrary",) * n_axes,
        vmem_limit_bytes=VMEM_LIMIT_BYTES)


def _cast_block_rows(rows, steps):
    for r in range(BF16_SUBLANES, rows + 1, BF16_SUBLANES):
        if rows % r == 0 and rows // r <= steps:
            return r
    raise ValueError(f"no row block for {rows} rows in {steps} steps")


def _call(body, name, grid, in_specs, args, out_spec, out_shape, scratch_shapes, casts):
    steps = math.prod(grid)

    def step_number(*idx):
        n = idx[0]
        for k, extent in zip(idx[1:], grid[1:]):
            n = n * extent + k
        return n

    cast_in, cast_out, cast_shapes = [], [], []
    for w, layer in casts:
        _, rows, cols = w.shape
        block = _cast_block_rows(rows, steps)
        last = rows // block - 1
        cast_in.append(pl.BlockSpec(
            (None, block, cols),
            lambda *idx, layer=layer, last=last: (layer, jnp.minimum(step_number(*idx), last), 0)))
        cast_out.append(pl.BlockSpec(
            (block, cols), lambda *idx, last=last: (jnp.minimum(step_number(*idx), last), 0)))
        cast_shapes.append(jax.ShapeDtypeStruct((rows, cols), BF16))
    n_in, n_cast = len(args), len(casts)

    def kern(*refs):
        in_refs = refs[:n_in]
        src_refs = refs[n_in:n_in + n_cast]
        out_ref = refs[n_in + n_cast]
        dst_refs = refs[n_in + n_cast + 1:n_in + 2 * n_cast + 1]
        scratch = refs[n_in + 2 * n_cast + 1:]
        body(*in_refs, out_ref, *scratch)
        for src, dst in zip(src_refs, dst_refs):
            dst[...] = src[...].astype(BF16)

    outs = pl.pallas_call(
        kern,
        grid=grid,
        in_specs=list(in_specs) + cast_in,
        out_specs=[out_spec] + cast_out,
        out_shape=[out_shape] + cast_shapes,
        scratch_shapes=scratch_shapes,
        compiler_params=_params(len(grid)),
        name=name,
    )(*args, *(w for w, _ in casts))
    return outs[0], list(outs[1:])


def _ffn_kernel(x_ref, xn_ref, g_ref, wg_ref, wu_ref, wd_ref, fg_ref, o_ref,
                h_ref, hn_ref, *, final):
    @pl.when(pl.program_id(0) == 0)
    def _():
        hn_ref[...] = _rmsnorm(x_ref[...], g_ref[...]).astype(BF16)

    h_ref[...] = hn_ref[...]
    d_ff = wg_ref.shape[1]
    n_chunks = d_ff // FFN_CHUNK
    tm = x_ref.shape[0]
    ahead = {c: r for r, c in enumerate(range(2, 2 + FFN_NORM_AHEAD_BLOCKS))}
    rows_ahead = tm // FFN_NORM_AHEAD_BLOCKS
    acc = x_ref[...]
    anchor = None
    for c in range(n_chunks):
        f0 = c * FFN_CHUNK
        h = h_ref[...]
        gate = _dot(h, wg_ref[:, f0:f0 + FFN_CHUNK])
        up = _dot(h, wu_ref[:, f0:f0 + FFN_CHUNK])
        act = jax.nn.silu(gate) * up
        if anchor is not None:
            act = act + anchor
            anchor = None
        acc = acc + _dot(act.astype(BF16), wd_ref[f0:f0 + FFN_CHUNK, :])
        if c in ahead:
            rows = slice(ahead[c] * rows_ahead, (ahead[c] + 1) * rows_ahead)
            hn = _rmsnorm(xn_ref[rows, :], g_ref[...]).astype(BF16)
            hn_ref[rows, :] = hn
            anchor = _zero_row_after(hn, FFN_CHUNK)
    if final:
        acc = _rmsnorm(acc, fg_ref[...])
    o_ref[...] = acc


def _ffn(x, g, weights, fg, layer, casts, *, final):
    wg, wu, wd = weights
    n, d = x.shape
    d_ff = wg.shape[1]
    assert n % FFN_TILE == 0 and d_ff % FFN_CHUNK == 0
    steps = n // FFN_TILE
    tile = pl.BlockSpec((FFN_TILE, d), lambda i: (i, 0))
    next_tile = pl.BlockSpec((FFN_TILE, d), lambda i: (jnp.minimum(i + 1, steps - 1), 0))
    return _call(
        functools.partial(_ffn_kernel, final=final),
        "ffn_final" if final else "ffn",
        (steps,),
        [tile, next_tile, _resident_layer(layer, (1, d)), _resident((d, d_ff)),
         _resident((d, d_ff)), _resident((d_ff, d)), _resident((1, d))],
        (x, x, g, wg, wu, wd, fg),
        tile, jax.ShapeDtypeStruct((n, d), F32),
        [pltpu.VMEM((FFN_TILE, d), BF16), pltpu.VMEM((FFN_TILE, d), BF16)], casts)


def _conv_kernel(x_ref, g_ref, win_ref, cw_ref, wout_ref, o_ref, cz_ref):
    tm = x_ref.shape[1]
    d = x_ref.shape[2]

    @pl.when(pl.program_id(1) == 0)
    def _():
        cz_ref[0:SUBLANES, :] = jnp.zeros((SUBLANES, d), F32)

    x = x_ref[0]
    h = _rmsnorm(x, g_ref[...]).astype(BF16)
    c_gate = _dot(h, win_ref[:, d:2 * d])
    z = _dot(h, win_ref[:, 2 * d:3 * d])
    cz = c_gate * z
    cz_ref[SUBLANES:SUBLANES + tm, :] = cz
    conv = cz * cw_ref[CONV_WIDTH - 1:CONV_WIDTH, :]
    for k in range(CONV_WIDTH - 1):
        shift = CONV_WIDTH - 1 - k
        conv = conv + cz_ref[SUBLANES - shift:SUBLANES - shift + tm, :] * cw_ref[k:k + 1, :]
    cz_ref[0:SUBLANES, :] = cz_ref[tm:tm + SUBLANES, :]
    b_gate = _dot(h, win_ref[:, 0:d])
    y = (b_gate * conv).astype(BF16)
    o_ref[0] = x + _dot(y, wout_ref[...])


def _conv_mixer(x, g, weights, conv_w, layer, j, casts):
    w_in, w_out = weights
    b, s, d = x.shape
    assert s % CONV_TILE == 0
    tile = pl.BlockSpec((1, CONV_TILE, d), lambda i, t: (i, t, 0))
    return _call(
        _conv_kernel, "conv_mixer", (b, s // CONV_TILE),
        [tile, _resident_layer(layer, (1, d)), _resident((d, 3 * d)),
         _resident_layer(j, (CONV_WIDTH, d)), _resident((d, d))],
        (x, g, w_in, conv_w, w_out),
        tile, jax.ShapeDtypeStruct((b, s, d), F32),
        [pltpu.VMEM((SUBLANES + CONV_TILE, d), F32)], casts)


def _gelu(x):
    return 0.5 * x * (1.0 + lax.erf(x * math.sqrt(0.5)))


def _sgu_kernel(x_ref, g_ref, win_ref, lng_ref, lnb_ref, ws_ref, bs_ref, wout_ref,
                o_ref, v_ref, s_ref):
    tm = x_ref.shape[0]
    x = x_ref[...]
    h = _rmsnorm(x, g_ref[...]).astype(BF16)

    v = _gelu(_dot(h, win_ref[:, SGU_HALF:2 * SGU_HALF]))
    mu = jnp.mean(v, axis=-1, keepdims=True)
    vc = v - mu
    var = jnp.mean(vc * vc, axis=-1, keepdims=True)
    v_ref[...] = (vc * lax.rsqrt(var + LN_EPS) * lng_ref[...] + lnb_ref[...]).astype(BF16)

    row = lax.broadcasted_iota(jnp.int32, (SGU_CHUNK, SGU_CHUNK), 0)
    col = lax.broadcasted_iota(jnp.int32, (SGU_CHUNK, SGU_CHUNK), 1)
    causal = row >= col
    for grp in range(SGU_GROUPS):
        w = jnp.where(causal, ws_ref[grp], 0.0).astype(BF16)
        bias = bs_ref[grp]
        lanes = slice(grp * SGU_GROUP_DIM, (grp + 1) * SGU_GROUP_DIM)
        for t0 in range(0, tm, SGU_CHUNK):
            rows = slice(t0, t0 + SGU_CHUNK)
            s_ref[rows, lanes] = _dot(w, v_ref[rows, lanes]) + bias

    u = _gelu(_dot(h, win_ref[:, 0:SGU_HALF]))
    y = (u * s_ref[...]).astype(BF16)
    o_ref[...] = x + _dot(y, wout_ref[...])


def _sgu_mixer(x, g, weights, ln_g, ln_b, w_s, b_s, layer, j, casts):
    w_in, w_out = weights
    n, d = x.shape
    assert n % SGU_TILE == 0 and SGU_TILE % SGU_CHUNK == 0
    tile = pl.BlockSpec((SGU_TILE, d), lambda i: (i, 0))
    return _call(
        _sgu_kernel, "sgu_mixer", (n // SGU_TILE,),
        [tile, _resident_layer(layer, (1, d)), _resident((d, 2 * SGU_HALF)),
         _resident_layer(j, (1, SGU_HALF)), _resident_layer(j, (1, SGU_HALF)),
         _resident_layer(j, (SGU_GROUPS, SGU_CHUNK, SGU_CHUNK)),
         _resident_layer(j, (SGU_GROUPS, SGU_CHUNK, 1)), _resident((SGU_HALF, d))],
        (x, g, w_in, ln_g, ln_b, w_s, b_s, w_out),
        tile, jax.ShapeDtypeStruct((n, d), F32),
        [pltpu.VMEM((SGU_TILE, SGU_HALF), BF16), pltpu.VMEM((SGU_TILE, SGU_HALF), F32)],
        casts)


def _rotary_halves(x1, x2, cos, sin):
    return x1 * cos - x2 * sin, x2 * cos + x1 * sin


def _ret_kernel(x_ref, g_ref, win_ref, wout_ref, cos_ref, sin_ref, o_ref,
                q_ref, k_ref, kd_ref, v_ref, y_ref, state_ref):
    tm = x_ref.shape[1]
    d = x_ref.shape[2]
    half = RET_QK_DIM // 2

    @pl.when(pl.program_id(1) == 0)
    def _():
        state_ref[...] = jnp.zeros(state_ref.shape, F32)

    x = x_ref[0]
    h = _rmsnorm(x, g_ref[...]).astype(BF16)
    cos = cos_ref[...]
    sin = sin_ref[...]

    pos = lax.rem(lax.broadcasted_iota(jnp.int32, (tm, 1), 0), RET_BLOCK).astype(F32)
    row = lax.broadcasted_iota(jnp.int32, (RET_BLOCK, RET_BLOCK), 0)
    col = lax.broadcasted_iota(jnp.int32, (RET_BLOCK, RET_BLOCK), 1)
    diff = (row - col).astype(F32)
    idx = lax.broadcasted_iota(jnp.int32, (RET_BLOCK, 1), 0).astype(F32)

    q = _dot(h, win_ref[:, 0:d])
    k = _dot(h, win_ref[:, d:2 * d])
    for hd in range(RET_HEADS):
        log_gamma = math.log(1.0 - 2.0 ** (-5.0 - hd))
        k_decay = jnp.exp((RET_BLOCK - 1.0 - pos) * log_gamma)
        lo = slice(hd * RET_QK_DIM, hd * RET_QK_DIM + half)
        hi = slice(hd * RET_QK_DIM + half, (hd + 1) * RET_QK_DIM)
        q1, q2 = _rotary_halves(q[:, lo], q[:, hi], cos, sin)
        q_ref[:, lo] = q1.astype(BF16)
        q_ref[:, hi] = q2.astype(BF16)
        k1, k2 = _rotary_halves(k[:, lo], k[:, hi], cos, sin)
        k1 = k1 * (RET_QK_DIM ** -0.5)
        k2 = k2 * (RET_QK_DIM ** -0.5)
        k_ref[:, lo] = k1.astype(BF16)
        k_ref[:, hi] = k2.astype(BF16)
        kd_ref[:, lo] = (k1 * k_decay).astype(BF16)
        kd_ref[:, hi] = (k2 * k_decay).astype(BF16)
    v_ref[...] = _dot(h, win_ref[:, 2 * d:4 * d]).astype(BF16)

    for hd in range(RET_HEADS):
        log_gamma = math.log(1.0 - 2.0 ** (-5.0 - hd))
        d_inner = jnp.where(diff >= 0, jnp.exp(jnp.maximum(diff, 0.0) * log_gamma), 0.0)
        q_decay = jnp.exp((idx + 1.0) * log_gamma)
        block_decay = math.exp(RET_BLOCK * log_gamma)
        qk_lanes = slice(hd * RET_QK_DIM, (hd + 1) * RET_QK_DIM)
        v_lanes = slice(hd * RET_V_DIM, (hd + 1) * RET_V_DIM)
        for t0 in range(0, tm, RET_BLOCK):
            rows = slice(t0, t0 + RET_BLOCK)
            qc = q_ref[rows, qk_lanes]
            vc = v_ref[rows, v_lanes]
            scores = lax.dot_general(qc, k_ref[rows, qk_lanes], (((1,), (1,)), ((), ())),
                                     preferred_element_type=F32) * d_inner
            inner = _dot(scores.astype(BF16), vc)
            state = state_ref[hd]
            cross = _dot(qc, state.astype(BF16)) * q_decay
            state_ref[hd] = state * block_decay + lax.dot_general(
                kd_ref[rows, qk_lanes], vc, (((0,), (0,)), ((), ())),
                preferred_element_type=F32)
            o = inner + cross
            o = o * lax.rsqrt(jnp.mean(o * o, axis=-1, keepdims=True) + RMS_EPS)
            y_ref[rows, v_lanes] = o

    gate = _dot(h, win_ref[:, 4 * d:6 * d])
    y = (jax.nn.silu(gate) * y_ref[...]).astype(BF16)
    o_ref[0] = x + _dot(y, wout_ref[...])


def _ret_mixer(x, g, weights, cos, sin, layer, casts):
    w_in, w_out = weights
    b, s, d = x.shape
    assert s % RET_TILE == 0 and RET_TILE % RET_BLOCK == 0
    tile = pl.BlockSpec((1, RET_TILE, d), lambda i, t: (i, t, 0))
    rope = pl.BlockSpec((RET_TILE, RET_QK_DIM // 2), lambda i, t: (t, 0))
    return _call(
        _ret_kernel, "ret_mixer", (b, s // RET_TILE),
        [tile, _resident_layer(layer, (1, d)), _resident((d, 6 * d)), _resident((2 * d, d)),
         rope, rope],
        (x, g, w_in, w_out, cos, sin),
        tile, jax.ShapeDtypeStruct((b, s, d), F32),
        [pltpu.VMEM((RET_TILE, d), BF16), pltpu.VMEM((RET_TILE, d), BF16),
         pltpu.VMEM((RET_TILE, d), BF16), pltpu.VMEM((RET_TILE, 2 * d), BF16),
         pltpu.VMEM((RET_TILE, 2 * d), F32),
         pltpu.VMEM((RET_HEADS, RET_QK_DIM, RET_V_DIM), F32)],
        casts)


def _rope_tables(seq):
    half = RET_QK_DIM // 2
    pos = jnp.arange(seq, dtype=F32)
    inv_freq = ROPE_BASE ** (-(jnp.arange(half, dtype=F32) / half))
    ang = pos[:, None] * inv_freq[None, :]
    return jnp.cos(ang), jnp.sin(ang)


def kernel(x, norm_mix_g, norm_ffn_g, final_norm_g, conv_w_in, conv_w, conv_w_out,
           sgu_w_in, sgu_ln_g, sgu_ln_b, sgu_w_s, sgu_b_s, sgu_w_out,
           ret_w_in, ret_w_out, ffn_w_gate, ffn_w_up, ffn_w_down):
    bsz, seq, d = x.shape
    depth = norm_mix_g.shape[0]
    cos, sin = _rope_tables(seq)
    rows = lambda v: v[:, None, :]
    norm_mix_g, norm_ffn_g = rows(norm_mix_g), rows(norm_ffn_g)
    final_norm_g = final_norm_g.reshape(1, d)
    sgu_ln_g, sgu_ln_b = rows(sgu_ln_g), rows(sgu_ln_b)
    sgu_b_s = sgu_b_s[..., None]

    mixer_weights = {0: (conv_w_in, conv_w_out), 1: (sgu_w_in, sgu_w_out),
                     2: (ret_w_in, ret_w_out)}
    stages = []
    for i in range(depth):
        j = i // N_MIXERS
        stages.append([(w, j) for w in mixer_weights[i % N_MIXERS]])
        stages.append([(w, i) for w in (ffn_w_gate, ffn_w_up, ffn_w_down)])
    stages.append([])

    weights = [w[layer].astype(BF16) for w, layer in stages[0]]
    for i in range(depth):
        kind = i % N_MIXERS
        j = i // N_MIXERS
        casts = stages[2 * i + 1]
        if kind == 0:
            x, weights = _conv_mixer(x, norm_mix_g, weights, conv_w, i, j, casts)
        elif kind == 1:
            x, weights = _sgu_mixer(x.reshape(bsz * seq, d), norm_mix_g, weights, sgu_ln_g,
                                    sgu_ln_b, sgu_w_s, sgu_b_s, i, j, casts)
            x = x.reshape(bsz, seq, d)
        else:
            x, weights = _ret_mixer(x, norm_mix_g, weights, cos, sin, i, casts)
        x, weights = _ffn(x.reshape(bsz * seq, d), norm_ffn_g, weights, final_norm_g, i,
                          stages[2 * i + 2], final=i == depth - 1)
        x = x.reshape(bsz, seq, d)
    return x
```

```python
import functools
import math

import jax
import jax.numpy as jnp
from jax import lax
from jax.experimental import pallas as pl
from jax.experimental.pallas import tpu as pltpu

F32 = jnp.float32
BF16 = jnp.bfloat16

D_MODEL = 1024
N_MIXERS = 3
RMS_EPS = 1e-6
LN_EPS = 1e-5
CONV_WIDTH = 3
SGU_CHUNK = 128
SGU_HALF = 2 * D_MODEL
SGU_GROUPS = 8
SGU_GROUP_DIM = SGU_HALF // SGU_GROUPS
RET_HEADS = 4
RET_QK_DIM = D_MODEL // RET_HEADS
RET_V_DIM = 2 * D_MODEL // RET_HEADS
ROPE_BASE = 10000.0
RET_BLOCK = 256

SUBLANES = 8
LANES = 128
BF16_SUBLANES = 16
VMEM_LIMIT_BYTES = 56 * 1024 * 1024

FFN_TILE = 1024
CONV_TILE = 1024
SGU_TILE = 1024
RET_TILE = 512
FFN_CHUNK = 256
NORM_AHEAD_BLOCKS = 4


def _rmsnorm(x, g):
    ms = jnp.mean(x * x, axis=-1, keepdims=True)
    return x * lax.rsqrt(ms + RMS_EPS) * g


def _dot(a, b):
    return jnp.dot(a, b, preferred_element_type=F32)


def _zero_row_after(value, width):
    bits = pltpu.bitcast(value, jnp.uint32)
    rows, cols = bits.shape
    tile = None
    for r0 in range(0, rows, SUBLANES):
        for c0 in range(0, cols, LANES):
            part = bits[r0:r0 + SUBLANES, c0:c0 + LANES]
            tile = part if tile is None else tile | part
    sixteen = jnp.uint32(16)
    tile = lax.shift_right_logical(lax.shift_right_logical(tile, sixteen), sixteen)
    row = pltpu.bitcast(tile, F32)[0:1, :]
    return jnp.concatenate([row] * (width // LANES), axis=1)


def _start_tile(first_step, x_tile, g_ref, h_ref, hn_ref):
    @pl.when(first_step)
    def _():
        hn_ref[...] = _rmsnorm(x_tile(), g_ref[...]).astype(BF16)

    h_ref[...] = hn_ref[...]


def _norm_block_ahead(xn_rows, g_ref, hn_ref, rows, width):
    hn = _rmsnorm(xn_rows, g_ref[...]).astype(BF16)
    hn_ref[rows, :] = hn
    return _zero_row_after(hn, width)


def _resident(shape):
    zeros = (0,) * len(shape)
    return pl.BlockSpec(shape, lambda *_: zeros, pipeline_mode=pl.Buffered(1))


def _resident_layer(layer, shape):
    index = (layer,) + (0,) * len(shape)
    return pl.BlockSpec((None,) + tuple(shape), lambda *_: index,
                        pipeline_mode=pl.Buffered(1))


def _params(n_axes):
    return pltpu.CompilerParams(
        dimension_semantics=("arbitrary",) * n_axes,
        vmem_limit_bytes=VMEM_LIMIT_BYTES)


def _cast_block_rows(rows, steps):
    for r in range(BF16_SUBLANES, rows + 1, BF16_SUBLANES):
        if rows % r == 0 and rows // r <= steps:
            return r
    raise ValueError(f"no row block for {rows} rows in {steps} steps")


def _call(body, name, grid, in_specs, args, out_spec, out_shape, scratch_shapes, casts):
    steps = math.prod(grid)

    def step_number(*idx):
        n = idx[0]
        for k, extent in zip(idx[1:], grid[1:]):
            n = n * extent + k
        return n

    cast_in, cast_out, cast_shapes = [], [], []
    for w, layer in casts:
        _, rows, cols = w.shape
        block = _cast_block_rows(rows, steps)
        last = rows // block - 1
        cast_in.append(pl.BlockSpec(
            (None, block, cols),
            lambda *idx, layer=layer, last=last: (layer, jnp.minimum(step_number(*idx), last), 0)))
        cast_out.append(pl.BlockSpec(
            (block, cols), lambda *idx, last=last: (jnp.minimum(step_number(*idx), last), 0)))
        cast_shapes.append(jax.ShapeDtypeStruct((rows, cols), BF16))
    n_in, n_cast = len(args), len(casts)

    def kern(*refs):
        in_refs = refs[:n_in]
        src_refs = refs[n_in:n_in + n_cast]
        out_ref = refs[n_in + n_cast]
        dst_refs = refs[n_in + n_cast + 1:n_in + 2 * n_cast + 1]
        scratch = refs[n_in + 2 * n_cast + 1:]
        body(*in_refs, out_ref, *scratch)
        for src, dst in zip(src_refs, dst_refs):
            dst[...] = src[...].astype(BF16)

    outs = pl.pallas_call(
        kern,
        grid=grid,
        in_specs=list(in_specs) + cast_in,
        out_specs=[out_spec] + cast_out,
        out_shape=[out_shape] + cast_shapes,
        scratch_shapes=scratch_shapes,
        compiler_params=_params(len(grid)),
        name=name,
    )(*args, *(w for w, _ in casts))
    return outs[0], list(outs[1:])


def _ffn_kernel(x_ref, xn_ref, g_ref, wg_ref, wu_ref, wd_ref, fg_ref, o_ref,
                h_ref, hn_ref, *, final):
    _start_tile(pl.program_id(0) == 0, lambda: x_ref[...], g_ref, h_ref, hn_ref)
    d_ff = wg_ref.shape[1]
    n_chunks = d_ff // FFN_CHUNK
    ahead = {c: r for r, c in enumerate(range(2, 2 + NORM_AHEAD_BLOCKS))}
    rows_ahead = x_ref.shape[0] // NORM_AHEAD_BLOCKS
    acc = x_ref[...]
    anchor = None
    for c in range(n_chunks):
        f0 = c * FFN_CHUNK
        h = h_ref[...]
        gate = _dot(h, wg_ref[:, f0:f0 + FFN_CHUNK])
        up = _dot(h, wu_ref[:, f0:f0 + FFN_CHUNK])
        act = jax.nn.silu(gate) * up
        if anchor is not None:
            act = act + anchor
            anchor = None
        acc = acc + _dot(act.astype(BF16), wd_ref[f0:f0 + FFN_CHUNK, :])
        if c in ahead:
            rows = slice(ahead[c] * rows_ahead, (ahead[c] + 1) * rows_ahead)
            anchor = _norm_block_ahead(xn_ref[rows, :], g_ref, hn_ref, rows, FFN_CHUNK)
    if final:
        acc = _rmsnorm(acc, fg_ref[...])
    o_ref[...] = acc


def _ffn(x, g, weights, fg, layer, casts, *, final):
    wg, wu, wd = weights
    n, d = x.shape
    d_ff = wg.shape[1]
    assert n % FFN_TILE == 0 and d_ff % FFN_CHUNK == 0
    steps = n // FFN_TILE
    tile = pl.BlockSpec((FFN_TILE, d), lambda i: (i, 0))
    next_tile = pl.BlockSpec((FFN_TILE, d), lambda i: (jnp.minimum(i + 1, steps - 1), 0))
    return _call(
        functools.partial(_ffn_kernel, final=final),
        "ffn_final" if final else "ffn",
        (steps,),
        [tile, next_tile, _resident_layer(layer, (1, d)), _resident((d, d_ff)),
         _resident((d, d_ff)), _resident((d_ff, d)), _resident((1, d))],
        (x, x, g, wg, wu, wd, fg),
        tile, jax.ShapeDtypeStruct((n, d), F32),
        [pltpu.VMEM((FFN_TILE, d), BF16), pltpu.VMEM((FFN_TILE, d), BF16)], casts)


def _conv_kernel(x_ref, g_ref, win_ref, cw_ref, wout_ref, o_ref, cz_ref):
    tm = x_ref.shape[1]
    d = x_ref.shape[2]

    @pl.when(pl.program_id(1) == 0)
    def _():
        cz_ref[0:SUBLANES, :] = jnp.zeros((SUBLANES, d), F32)

    x = x_ref[0]
    h = _rmsnorm(x, g_ref[...]).astype(BF16)
    c_gate = _dot(h, win_ref[:, d:2 * d])
    z = _dot(h, win_ref[:, 2 * d:3 * d])
    cz = c_gate * z
    cz_ref[SUBLANES:SUBLANES + tm, :] = cz
    conv = cz * cw_ref[CONV_WIDTH - 1:CONV_WIDTH, :]
    for k in range(CONV_WIDTH - 1):
        shift = CONV_WIDTH - 1 - k
        conv = conv + cz_ref[SUBLANES - shift:SUBLANES - shift + tm, :] * cw_ref[k:k + 1, :]
    cz_ref[0:SUBLANES, :] = cz_ref[tm:tm + SUBLANES, :]
    b_gate = _dot(h, win_ref[:, 0:d])
    y = (b_gate * conv).astype(BF16)
    o_ref[0] = x + _dot(y, wout_ref[...])


def _conv_mixer(x, g, weights, conv_w, layer, j, casts):
    w_in, w_out = weights
    b, s, d = x.shape
    assert s % CONV_TILE == 0
    tile = pl.BlockSpec((1, CONV_TILE, d), lambda i, t: (i, t, 0))
    return _call(
        _conv_kernel, "conv_mixer", (b, s // CONV_TILE),
        [tile, _resident_layer(layer, (1, d)), _resident((d, 3 * d)),
         _resident_layer(j, (CONV_WIDTH, d)), _resident((d, d))],
        (x, g, w_in, conv_w, w_out),
        tile, jax.ShapeDtypeStruct((b, s, d), F32),
        [pltpu.VMEM((SUBLANES + CONV_TILE, d), F32)], casts)


def _gelu(x):
    return 0.5 * x * (1.0 + lax.erf(x * math.sqrt(0.5)))


def _sgu_kernel(x_ref, g_ref, win_ref, lng_ref, lnb_ref, ws_ref, bs_ref, wout_ref,
                o_ref, v_ref, s_ref):
    tm = x_ref.shape[0]
    x = x_ref[...]
    h = _rmsnorm(x, g_ref[...]).astype(BF16)

    v = _gelu(_dot(h, win_ref[:, SGU_HALF:2 * SGU_HALF]))
    mu = jnp.mean(v, axis=-1, keepdims=True)
    vc = v - mu
    var = jnp.mean(vc * vc, axis=-1, keepdims=True)
    v_ref[...] = (vc * lax.rsqrt(var + LN_EPS) * lng_ref[...] + lnb_ref[...]).astype(BF16)

    row = lax.broadcasted_iota(jnp.int32, (SGU_CHUNK, SGU_CHUNK), 0)
    col = lax.broadcasted_iota(jnp.int32, (SGU_CHUNK, SGU_CHUNK), 1)
    causal = row >= col
    for grp in range(SGU_GROUPS):
        w = jnp.where(causal, ws_ref[grp], 0.0).astype(BF16)
        bias = bs_ref[grp]
        lanes = slice(grp * SGU_GROUP_DIM, (grp + 1) * SGU_GROUP_DIM)
        for t0 in range(0, tm, SGU_CHUNK):
            rows = slice(t0, t0 + SGU_CHUNK)
            s_ref[rows, lanes] = _dot(w, v_ref[rows, lanes]) + bias

    u = _gelu(_dot(h, win_ref[:, 0:SGU_HALF]))
    y = (u * s_ref[...]).astype(BF16)
    o_ref[...] = x + _dot(y, wout_ref[...])


def _sgu_mixer(x, g, weights, ln_g, ln_b, w_s, b_s, layer, j, casts):
    w_in, w_out = weights
    n, d = x.shape
    assert n % SGU_TILE == 0 and SGU_TILE % SGU_CHUNK == 0
    tile = pl.BlockSpec((SGU_TILE, d), lambda i: (i, 0))
    return _call(
        _sgu_kernel, "sgu_mixer", (n // SGU_TILE,),
        [tile, _resident_layer(layer, (1, d)), _resident((d, 2 * SGU_HALF)),
         _resident_layer(j, (1, SGU_HALF)), _resident_layer(j, (1, SGU_HALF)),
         _resident_layer(j, (SGU_GROUPS, SGU_CHUNK, SGU_CHUNK)),
         _resident_layer(j, (SGU_GROUPS, SGU_CHUNK, 1)), _resident((SGU_HALF, d))],
        (x, g, w_in, ln_g, ln_b, w_s, b_s, w_out),
        tile, jax.ShapeDtypeStruct((n, d), F32),
        [pltpu.VMEM((SGU_TILE, SGU_HALF), BF16), pltpu.VMEM((SGU_TILE, SGU_HALF), F32)],
        casts)


def _rotary_halves(x1, x2, cos, sin):
    return x1 * cos - x2 * sin, x2 * cos + x1 * sin


def _ret_kernel(x_ref, g_ref, win_ref, wout_ref, cos_ref, sin_ref, o_ref,
                q_ref, k_ref, kd_ref, v_ref, y_ref, state_ref):
    tm = x_ref.shape[1]
    d = x_ref.shape[2]
    half = RET_QK_DIM // 2

    @pl.when(pl.program_id(1) == 0)
    def _():
        state_ref[...] = jnp.zeros(state_ref.shape, F32)

    x = x_ref[0]
    h = _rmsnorm(x, g_ref[...]).astype(BF16)
    cos = cos_ref[...]
    sin = sin_ref[...]

    pos = lax.rem(lax.broadcasted_iota(jnp.int32, (tm, 1), 0), RET_BLOCK).astype(F32)
    row = lax.broadcasted_iota(jnp.int32, (RET_BLOCK, RET_BLOCK), 0)
    col = lax.broadcasted_iota(jnp.int32, (RET_BLOCK, RET_BLOCK), 1)
    diff = (row - col).astype(F32)
    idx = lax.broadcasted_iota(jnp.int32, (RET_BLOCK, 1), 0).astype(F32)

    q = _dot(h, win_ref[:, 0:d])
    k = _dot(h, win_ref[:, d:2 * d])
    for hd in range(RET_HEADS):
        log_gamma = math.log(1.0 - 2.0 ** (-5.0 - hd))
        k_decay = jnp.exp((RET_BLOCK - 1.0 - pos) * log_gamma)
        lo = slice(hd * RET_QK_DIM, hd * RET_QK_DIM + half)
        hi = slice(hd * RET_QK_DIM + half, (hd + 1) * RET_QK_DIM)
        q1, q2 = _rotary_halves(q[:, lo], q[:, hi], cos, sin)
        q_ref[:, lo] = q1.astype(BF16)
        q_ref[:, hi] = q2.astype(BF16)
        k1, k2 = _rotary_halves(k[:, lo], k[:, hi], cos, sin)
        k1 = k1 * (RET_QK_DIM ** -0.5)
        k2 = k2 * (RET_QK_DIM ** -0.5)
        k_ref[:, lo] = k1.astype(BF16)
        k_ref[:, hi] = k2.astype(BF16)
        kd_ref[:, lo] = (k1 * k_decay).astype(BF16)
        kd_ref[:, hi] = (k2 * k_decay).astype(BF16)
    v_ref[...] = _dot(h, win_ref[:, 2 * d:4 * d]).astype(BF16)

    for hd in range(RET_HEADS):
        log_gamma = math.log(1.0 - 2.0 ** (-5.0 - hd))
        d_inner = jnp.where(diff >= 0, jnp.exp(jnp.maximum(diff, 0.0) * log_gamma), 0.0)
        q_decay = jnp.exp((idx + 1.0) * log_gamma)
        block_decay = math.exp(RET_BLOCK * log_gamma)
        qk_lanes = slice(hd * RET_QK_DIM, (hd + 1) * RET_QK_DIM)
        v_lanes = slice(hd * RET_V_DIM, (hd + 1) * RET_V_DIM)
        for t0 in range(0, tm, RET_BLOCK):
            rows = slice(t0, t0 + RET_BLOCK)
            qc = q_ref[rows, qk_lanes]
            vc = v_ref[rows, v_lanes]
            scores = lax.dot_general(qc, k_ref[rows, qk_lanes], (((1,), (1,)), ((), ())),
                                     preferred_element_type=F32) * d_inner
            inner = _dot(scores.astype(BF16), vc)
            state = state_ref[hd]
            cross = _dot(qc, state.astype(BF16)) * q_decay
            state_ref[hd] = state * block_decay + lax.dot_general(
                kd_ref[rows, qk_lanes], vc, (((0,), (0,)), ((), ())),
                preferred_element_type=F32)
            o = inner + cross
            o = o * lax.rsqrt(jnp.mean(o * o, axis=-1, keepdims=True) + RMS_EPS)
            y_ref[rows, v_lanes] = o

    gate = _dot(h, win_ref[:, 4 * d:6 * d])
    y = (jax.nn.silu(gate) * y_ref[...]).astype(BF16)
    o_ref[0] = x + _dot(y, wout_ref[...])


def _ret_mixer(x, g, weights, cos, sin, layer, casts):
    w_in, w_out = weights
    b, s, d = x.shape
    assert s % RET_TILE == 0 and RET_TILE % RET_BLOCK == 0
    tile = pl.BlockSpec((1, RET_TILE, d), lambda i, t: (i, t, 0))
    rope = pl.BlockSpec((RET_TILE, RET_QK_DIM // 2), lambda i, t: (t, 0))
    return _call(
        _ret_kernel, "ret_mixer", (b, s // RET_TILE),
        [tile, _resident_layer(layer, (1, d)), _resident((d, 6 * d)), _resident((2 * d, d)),
         rope, rope],
        (x, g, w_in, w_out, cos, sin),
        tile, jax.ShapeDtypeStruct((b, s, d), F32),
        [pltpu.VMEM((RET_TILE, d), BF16), pltpu.VMEM((RET_TILE, d), BF16),
         pltpu.VMEM((RET_TILE, d), BF16), pltpu.VMEM((RET_TILE, 2 * d), BF16),
         pltpu.VMEM((RET_TILE, 2 * d), F32),
         pltpu.VMEM((RET_HEADS, RET_QK_DIM, RET_V_DIM), F32)],
        casts)


def _rope_tables(seq):
    half = RET_QK_DIM // 2
    pos = jnp.arange(seq, dtype=F32)
    inv_freq = ROPE_BASE ** (-(jnp.arange(half, dtype=F32) / half))
    ang = pos[:, None] * inv_freq[None, :]
    return jnp.cos(ang), jnp.sin(ang)


def kernel(x, norm_mix_g, norm_ffn_g, final_norm_g, conv_w_in, conv_w, conv_w_out,
           sgu_w_in, sgu_ln_g, sgu_ln_b, sgu_w_s, sgu_b_s, sgu_w_out,
           ret_w_in, ret_w_out, ffn_w_gate, ffn_w_up, ffn_w_down):
    bsz, seq, d = x.shape
    depth = norm_mix_g.shape[0]
    cos, sin = _rope_tables(seq)
    rows = lambda v: v[:, None, :]
    norm_mix_g, norm_ffn_g = rows(norm_mix_g), rows(norm_ffn_g)
    final_norm_g = final_norm_g.reshape(1, d)
    sgu_ln_g, sgu_ln_b = rows(sgu_ln_g), rows(sgu_ln_b)
    sgu_b_s = sgu_b_s[..., None]

    mixer_weights = {0: (conv_w_in, conv_w_out), 1: (sgu_w_in, sgu_w_out),
                     2: (ret_w_in, ret_w_out)}
    stages = []
    for i in range(depth):
        j = i // N_MIXERS
        stages.append([(w, j) for w in mixer_weights[i % N_MIXERS]])
        stages.append([(w, i) for w in (ffn_w_gate, ffn_w_up, ffn_w_down)])
    stages.append([])

    weights = [w[layer].astype(BF16) for w, layer in stages[0]]
    for i in range(depth):
        kind = i % N_MIXERS
        j = i // N_MIXERS
        casts = stages[2 * i + 1]
        if kind == 0:
            x, weights = _conv_mixer(x, norm_mix_g, weights, conv_w, i, j, casts)
        elif kind == 1:
            x, weights = _sgu_mixer(x.reshape(bsz * seq, d), norm_mix_g, weights, sgu_ln_g,
                                    sgu_ln_b, sgu_w_s, sgu_b_s, i, j, casts)
            x = x.reshape(bsz, seq, d)
        else:
            x, weights = _ret_mixer(x, norm_mix_g, weights, cos, sin, i, casts)
        x, weights = _ffn(x.reshape(bsz * seq, d), norm_ffn_g, weights, final_norm_g, i,
                          stages[2 * i + 2], final=i == depth - 1)
        x = x.reshape(bsz, seq, d)
    return x
```

```python
import functools
import math

import jax
import jax.numpy as jnp
from jax import lax
from jax.experimental import pallas as pl
from jax.experimental.pallas import tpu as pltpu

F32 = jnp.float32
BF16 = jnp.bfloat16

D_MODEL = 1024
N_MIXERS = 3
RMS_EPS = 1e-6
LN_EPS = 1e-5
CONV_WIDTH = 3
SGU_CHUNK = 128
SGU_HALF = 2 * D_MODEL
SGU_GROUPS = 8
SGU_GROUP_DIM = SGU_HALF // SGU_GROUPS
RET_HEADS = 4
RET_QK_DIM = D_MODEL // RET_HEADS
RET_V_DIM = 2 * D_MODEL // RET_HEADS
ROPE_BASE = 10000.0
RET_BLOCK = 256

SUBLANES = 8
LANES = 128
BF16_SUBLANES = 16
VMEM_LIMIT_BYTES = 60 * 1024 * 1024

FFN_TILE = 1024
CONV_TILE = 1024
SGU_TILE = 1024
RET_TILE = 1024
FFN_CHUNK = 256
NORM_AHEAD_BLOCKS = 4


def _rmsnorm(x, g):
    ms = jnp.mean(x * x, axis=-1, keepdims=True)
    return x * lax.rsqrt(ms + RMS_EPS) * g


def _dot(a, b):
    return jnp.dot(a, b, preferred_element_type=F32)


def _zero_row_after(value, width):
    bits = pltpu.bitcast(value, jnp.uint32)
    rows, cols = bits.shape
    tile = None
    for r0 in range(0, rows, SUBLANES):
        for c0 in range(0, cols, LANES):
            part = bits[r0:r0 + SUBLANES, c0:c0 + LANES]
            tile = part if tile is None else tile | part
    sixteen = jnp.uint32(16)
    tile = lax.shift_right_logical(lax.shift_right_logical(tile, sixteen), sixteen)
    row = pltpu.bitcast(tile, F32)[0:1, :]
    return jnp.concatenate([row] * (width // LANES), axis=1)


def _start_tile(first_step, x_tile, g_ref, h_ref, hn_ref):
    @pl.when(first_step)
    def _():
        hn_ref[...] = _rmsnorm(x_tile(), g_ref[...]).astype(BF16)

    h_ref[...] = hn_ref[...]


def _norm_block_ahead(xn_rows, g_ref, hn_ref, rows, width):
    hn = _rmsnorm(xn_rows, g_ref[...]).astype(BF16)
    hn_ref[rows, :] = hn
    return _zero_row_after(hn, width)


def _resident(shape):
    zeros = (0,) * len(shape)
    return pl.BlockSpec(shape, lambda *_: zeros, pipeline_mode=pl.Buffered(1))


def _resident_layer(layer, shape):
    index = (layer,) + (0,) * len(shape)
    return pl.BlockSpec((None,) + tuple(shape), lambda *_: index,
                        pipeline_mode=pl.Buffered(1))


def _params(n_axes):
    return pltpu.CompilerParams(
        dimension_semantics=("arbitrary",) * n_axes,
        vmem_limit_bytes=VMEM_LIMIT_BYTES)


def _cast_block_rows(rows, steps):
    for r in range(BF16_SUBLANES, rows + 1, BF16_SUBLANES):
        if rows % r == 0 and rows // r <= steps:
            return r
    raise ValueError(f"no row block for {rows} rows in {steps} steps")


def _call(body, name, grid, in_specs, args, out_spec, out_shape, scratch_shapes, casts):
    steps = math.prod(grid)

    def step_number(*idx):
        n = idx[0]
        for k, extent in zip(idx[1:], grid[1:]):
            n = n * extent + k
        return n

    cast_in, cast_out, cast_shapes = [], [], []
    for w, layer in casts:
        _, rows, cols = w.shape
        block = _cast_block_rows(rows, steps)
        last = rows // block - 1
        cast_in.append(pl.BlockSpec(
            (None, block, cols),
            lambda *idx, layer=layer, last=last: (layer, jnp.minimum(step_number(*idx), last), 0)))
        cast_out.append(pl.BlockSpec(
            (block, cols), lambda *idx, last=last: (jnp.minimum(step_number(*idx), last), 0)))
        cast_shapes.append(jax.ShapeDtypeStruct((rows, cols), BF16))
    n_in, n_cast = len(args), len(casts)

    def kern(*refs):
        in_refs = refs[:n_in]
        src_refs = refs[n_in:n_in + n_cast]
        out_ref = refs[n_in + n_cast]
        dst_refs = refs[n_in + n_cast + 1:n_in + 2 * n_cast + 1]
        scratch = refs[n_in + 2 * n_cast + 1:]
        body(*in_refs, out_ref, *scratch)
        for src, dst in zip(src_refs, dst_refs):
            dst[...] = src[...].astype(BF16)

    outs = pl.pallas_call(
        kern,
        grid=grid,
        in_specs=list(in_specs) + cast_in,
        out_specs=[out_spec] + cast_out,
        out_shape=[out_shape] + cast_shapes,
        scratch_shapes=scratch_shapes,
        compiler_params=_params(len(grid)),
        name=name,
    )(*args, *(w for w, _ in casts))
    return outs[0], list(outs[1:])


def _ffn_kernel(x_ref, xn_ref, g_ref, wg_ref, wu_ref, wd_ref, fg_ref, o_ref,
                h_ref, hn_ref, *, final):
    _start_tile(pl.program_id(0) == 0, lambda: x_ref[...], g_ref, h_ref, hn_ref)
    d_ff = wg_ref.shape[1]
    n_chunks = d_ff // FFN_CHUNK
    ahead = {c: r for r, c in enumerate(range(2, 2 + NORM_AHEAD_BLOCKS))}
    rows_ahead = x_ref.shape[0] // NORM_AHEAD_BLOCKS
    acc = x_ref[...]
    anchor = None
    for c in range(n_chunks):
        f0 = c * FFN_CHUNK
        h = h_ref[...]
        gate = _dot(h, wg_ref[:, f0:f0 + FFN_CHUNK])
        up = _dot(h, wu_ref[:, f0:f0 + FFN_CHUNK])
        act = jax.nn.silu(gate) * up
        if anchor is not None:
            act = act + anchor
            anchor = None
        acc = acc + _dot(act.astype(BF16), wd_ref[f0:f0 + FFN_CHUNK, :])
        if c in ahead:
            rows = slice(ahead[c] * rows_ahead, (ahead[c] + 1) * rows_ahead)
            anchor = _norm_block_ahead(xn_ref[rows, :], g_ref, hn_ref, rows, FFN_CHUNK)
    if final:
        acc = _rmsnorm(acc, fg_ref[...])
    o_ref[...] = acc


def _ffn(x, g, weights, fg, layer, casts, *, final):
    wg, wu, wd = weights
    n, d = x.shape
    d_ff = wg.shape[1]
    assert n % FFN_TILE == 0 and d_ff % FFN_CHUNK == 0
    steps = n // FFN_TILE
    tile = pl.BlockSpec((FFN_TILE, d), lambda i: (i, 0))
    next_tile = pl.BlockSpec((FFN_TILE, d), lambda i: (jnp.minimum(i + 1, steps - 1), 0))
    return _call(
        functools.partial(_ffn_kernel, final=final),
        "ffn_final" if final else "ffn",
        (steps,),
        [tile, next_tile, _resident_layer(layer, (1, d)), _resident((d, d_ff)),
         _resident((d, d_ff)), _resident((d_ff, d)), _resident((1, d))],
        (x, x, g, wg, wu, wd, fg),
        tile, jax.ShapeDtypeStruct((n, d), F32),
        [pltpu.VMEM((FFN_TILE, d), BF16), pltpu.VMEM((FFN_TILE, d), BF16)], casts)


def _conv_kernel(x_ref, g_ref, win_ref, cw_ref, wout_ref, o_ref, cz_ref):
    tm = x_ref.shape[1]
    d = x_ref.shape[2]

    @pl.when(pl.program_id(1) == 0)
    def _():
        cz_ref[0:SUBLANES, :] = jnp.zeros((SUBLANES, d), F32)

    x = x_ref[0]
    h = _rmsnorm(x, g_ref[...]).astype(BF16)
    c_gate = _dot(h, win_ref[:, d:2 * d])
    z = _dot(h, win_ref[:, 2 * d:3 * d])
    cz = c_gate * z
    cz_ref[SUBLANES:SUBLANES + tm, :] = cz
    conv = cz * cw_ref[CONV_WIDTH - 1:CONV_WIDTH, :]
    for k in range(CONV_WIDTH - 1):
        shift = CONV_WIDTH - 1 - k
        conv = conv + cz_ref[SUBLANES - shift:SUBLANES - shift + tm, :] * cw_ref[k:k + 1, :]
    cz_ref[0:SUBLANES, :] = cz_ref[tm:tm + SUBLANES, :]
    b_gate = _dot(h, win_ref[:, 0:d])
    y = (b_gate * conv).astype(BF16)
    o_ref[0] = x + _dot(y, wout_ref[...])


def _conv_mixer(x, g, weights, conv_w, layer, j, casts):
    w_in, w_out = weights
    b, s, d = x.shape
    assert s % CONV_TILE == 0
    tile = pl.BlockSpec((1, CONV_TILE, d), lambda i, t: (i, t, 0))
    return _call(
        _conv_kernel, "conv_mixer", (b, s // CONV_TILE),
        [tile, _resident_layer(layer, (1, d)), _resident((d, 3 * d)),
         _resident_layer(j, (CONV_WIDTH, d)), _resident((d, d))],
        (x, g, w_in, conv_w, w_out),
        tile, jax.ShapeDtypeStruct((b, s, d), F32),
        [pltpu.VMEM((SUBLANES + CONV_TILE, d), F32)], casts)


def _gelu(x):
    return 0.5 * x * (1.0 + lax.erf(x * math.sqrt(0.5)))


def _sgu_kernel(x_ref, g_ref, win_ref, lng_ref, lnb_ref, ws_ref, bs_ref, wout_ref,
                o_ref, v_ref, s_ref):
    tm = x_ref.shape[0]
    x = x_ref[...]
    h = _rmsnorm(x, g_ref[...]).astype(BF16)

    v = _gelu(_dot(h, win_ref[:, SGU_HALF:2 * SGU_HALF]))
    mu = jnp.mean(v, axis=-1, keepdims=True)
    vc = v - mu
    var = jnp.mean(vc * vc, axis=-1, keepdims=True)
    v_ref[...] = (vc * lax.rsqrt(var + LN_EPS) * lng_ref[...] + lnb_ref[...]).astype(BF16)

    row = lax.broadcasted_iota(jnp.int32, (SGU_CHUNK, SGU_CHUNK), 0)
    col = lax.broadcasted_iota(jnp.int32, (SGU_CHUNK, SGU_CHUNK), 1)
    causal = row >= col
    for grp in range(SGU_GROUPS):
        w = jnp.where(causal, ws_ref[grp], 0.0).astype(BF16)
        bias = bs_ref[grp]
        lanes = slice(grp * SGU_GROUP_DIM, (grp + 1) * SGU_GROUP_DIM)
        for t0 in range(0, tm, SGU_CHUNK):
            rows = slice(t0, t0 + SGU_CHUNK)
            s_ref[rows, lanes] = _dot(w, v_ref[rows, lanes]) + bias

    u = _gelu(_dot(h, win_ref[:, 0:SGU_HALF]))
    y = (u * s_ref[...]).astype(BF16)
    o_ref[...] = x + _dot(y, wout_ref[...])


def _sgu_mixer(x, g, weights, ln_g, ln_b, w_s, b_s, layer, j, casts):
    w_in, w_out = weights
    n, d = x.shape
    assert n % SGU_TILE == 0 and SGU_TILE % SGU_CHUNK == 0
    tile = pl.BlockSpec((SGU_TILE, d), lambda i: (i, 0))
    return _call(
        _sgu_kernel, "sgu_mixer", (n // SGU_TILE,),
        [tile, _resident_layer(layer, (1, d)), _resident((d, 2 * SGU_HALF)),
         _resident_layer(j, (1, SGU_HALF)), _resident_layer(j, (1, SGU_HALF)),
         _resident_layer(j, (SGU_GROUPS, SGU_CHUNK, SGU_CHUNK)),
         _resident_layer(j, (SGU_GROUPS, SGU_CHUNK, 1)), _resident((SGU_HALF, d))],
        (x, g, w_in, ln_g, ln_b, w_s, b_s, w_out),
        tile, jax.ShapeDtypeStruct((n, d), F32),
        [pltpu.VMEM((SGU_TILE, SGU_HALF), BF16), pltpu.VMEM((SGU_TILE, SGU_HALF), F32)],
        casts)


def _rotary_halves(x1, x2, cos, sin):
    return x1 * cos - x2 * sin, x2 * cos + x1 * sin


def _ret_kernel(x_ref, g_ref, win_ref, wout_ref, cos_ref, sin_ref, o_ref,
                q_ref, k_ref, kd_ref, v_ref, y_ref, state_ref):
    tm = x_ref.shape[1]
    d = x_ref.shape[2]
    half = RET_QK_DIM // 2

    @pl.when(pl.program_id(1) == 0)
    def _():
        state_ref[...] = jnp.zeros(state_ref.shape, F32)

    x = x_ref[0]
    h = _rmsnorm(x, g_ref[...]).astype(BF16)
    cos = cos_ref[...]
    sin = sin_ref[...]

    pos = lax.rem(lax.broadcasted_iota(jnp.int32, (tm, 1), 0), RET_BLOCK).astype(F32)
    row = lax.broadcasted_iota(jnp.int32, (RET_BLOCK, RET_BLOCK), 0)
    col = lax.broadcasted_iota(jnp.int32, (RET_BLOCK, RET_BLOCK), 1)
    diff = (row - col).astype(F32)
    idx = lax.broadcasted_iota(jnp.int32, (RET_BLOCK, 1), 0).astype(F32)

    q = _dot(h, win_ref[:, 0:d])
    k = _dot(h, win_ref[:, d:2 * d])
    for hd in range(RET_HEADS):
        log_gamma = math.log(1.0 - 2.0 ** (-5.0 - hd))
        k_decay = jnp.exp((RET_BLOCK - 1.0 - pos) * log_gamma)
        lo = slice(hd * RET_QK_DIM, hd * RET_QK_DIM + half)
        hi = slice(hd * RET_QK_DIM + half, (hd + 1) * RET_QK_DIM)
        q1, q2 = _rotary_halves(q[:, lo], q[:, hi], cos, sin)
        q_ref[:, lo] = q1.astype(BF16)
        q_ref[:, hi] = q2.astype(BF16)
        k1, k2 = _rotary_halves(k[:, lo], k[:, hi], cos, sin)
        k1 = k1 * (RET_QK_DIM ** -0.5)
        k2 = k2 * (RET_QK_DIM ** -0.5)
        k_ref[:, lo] = k1.astype(BF16)
        k_ref[:, hi] = k2.astype(BF16)
        kd_ref[:, lo] = (k1 * k_decay).astype(BF16)
        kd_ref[:, hi] = (k2 * k_decay).astype(BF16)
    v_ref[...] = _dot(h, win_ref[:, 2 * d:4 * d]).astype(BF16)

    for hd in range(RET_HEADS):
        log_gamma = math.log(1.0 - 2.0 ** (-5.0 - hd))
        d_inner = jnp.where(diff >= 0, jnp.exp(jnp.maximum(diff, 0.0) * log_gamma), 0.0)
        q_decay = jnp.exp((idx + 1.0) * log_gamma)
        block_decay = math.exp(RET_BLOCK * log_gamma)
        qk_lanes = slice(hd * RET_QK_DIM, (hd + 1) * RET_QK_DIM)
        v_lanes = slice(hd * RET_V_DIM, (hd + 1) * RET_V_DIM)
        for t0 in range(0, tm, RET_BLOCK):
            rows = slice(t0, t0 + RET_BLOCK)
            qc = q_ref[rows, qk_lanes]
            vc = v_ref[rows, v_lanes]
            scores = lax.dot_general(qc, k_ref[rows, qk_lanes], (((1,), (1,)), ((), ())),
                                     preferred_element_type=F32) * d_inner
            inner = _dot(scores.astype(BF16), vc)
            state = state_ref[hd]
            cross = _dot(qc, state.astype(BF16)) * q_decay
            state_ref[hd] = state * block_decay + lax.dot_general(
                kd_ref[rows, qk_lanes], vc, (((0,), (0,)), ((), ())),
                preferred_element_type=F32)
            o = inner + cross
            o = o * lax.rsqrt(jnp.mean(o * o, axis=-1, keepdims=True) + RMS_EPS)
            y_ref[rows, v_lanes] = o

    gate = _dot(h, win_ref[:, 4 * d:6 * d])
    y = (jax.nn.silu(gate) * y_ref[...]).astype(BF16)
    o_ref[0] = x + _dot(y, wout_ref[...])


def _ret_mixer(x, g, weights, cos, sin, layer, casts):
    w_in, w_out = weights
    b, s, d = x.shape
    assert s % RET_TILE == 0 and RET_TILE % RET_BLOCK == 0
    tile = pl.BlockSpec((1, RET_TILE, d), lambda i, t: (i, t, 0))
    rope = pl.BlockSpec((RET_TILE, RET_QK_DIM // 2), lambda i, t: (t, 0))
    return _call(
        _ret_kernel, "ret_mixer", (b, s // RET_TILE),
        [tile, _resident_layer(layer, (1, d)), _resident((d, 6 * d)), _resident((2 * d, d)),
         rope, rope],
        (x, g, w_in, w_out, cos, sin),
        tile, jax.ShapeDtypeStruct((b, s, d), F32),
        [pltpu.VMEM((RET_TILE, d), BF16), pltpu.VMEM((RET_TILE, d), BF16),
         pltpu.VMEM((RET_TILE, d), BF16), pltpu.VMEM((RET_TILE, 2 * d), BF16),
         pltpu.VMEM((RET_TILE, 2 * d), F32),
         pltpu.VMEM((RET_HEADS, RET_QK_DIM, RET_V_DIM), F32)],
        casts)


def _rope_tables(seq):
    half = RET_QK_DIM // 2
    pos = jnp.arange(seq, dtype=F32)
    inv_freq = ROPE_BASE ** (-(jnp.arange(half, dtype=F32) / half))
    ang = pos[:, None] * inv_freq[None, :]
    return jnp.cos(ang), jnp.sin(ang)


def kernel(x, norm_mix_g, norm_ffn_g, final_norm_g, conv_w_in, conv_w, conv_w_out,
           sgu_w_in, sgu_ln_g, sgu_ln_b, sgu_w_s, sgu_b_s, sgu_w_out,
           ret_w_in, ret_w_out, ffn_w_gate, ffn_w_up, ffn_w_down):
    bsz, seq, d = x.shape
    depth = norm_mix_g.shape[0]
    cos, sin = _rope_tables(seq)
    rows = lambda v: v[:, None, :]
    norm_mix_g, norm_ffn_g = rows(norm_mix_g), rows(norm_ffn_g)
    final_norm_g = final_norm_g.reshape(1, d)
    sgu_ln_g, sgu_ln_b = rows(sgu_ln_g), rows(sgu_ln_b)
    sgu_b_s = sgu_b_s[..., None]

    mixer_weights = {0: (conv_w_in, conv_w_out), 1: (sgu_w_in, sgu_w_out),
                     2: (ret_w_in, ret_w_out)}
    stages = []
    for i in range(depth):
        j = i // N_MIXERS
        stages.append([(w, j) for w in mixer_weights[i % N_MIXERS]])
        stages.append([(w, i) for w in (ffn_w_gate, ffn_w_up, ffn_w_down)])
    stages.append([])

    weights = [w[layer].astype(BF16) for w, layer in stages[0]]
    for i in range(depth):
        kind = i % N_MIXERS
        j = i // N_MIXERS
        casts = stages[2 * i + 1]
        if kind == 0:
            x, weights = _conv_mixer(x, norm_mix_g, weights, conv_w, i, j, casts)
        elif kind == 1:
            x, weights = _sgu_mixer(x.reshape(bsz * seq, d), norm_mix_g, weights, sgu_ln_g,
                                    sgu_ln_b, sgu_w_s, sgu_b_s, i, j, casts)
            x = x.reshape(bsz, seq, d)
        else:
            x, weights = _ret_mixer(x, norm_mix_g, weights, cos, sin, i, casts)
        x, weights = _ffn(x.reshape(bsz * seq, d), norm_ffn_g, weights, final_norm_g, i,
                          stages[2 * i + 2], final=i == depth - 1)
        x = x.reshape(bsz, seq, d)
    return x
```

```python
import functools
import math

import jax
import jax.numpy as jnp
from jax import lax
from jax.experimental import pallas as pl
from jax.experimental.pallas import tpu as pltpu

F32 = jnp.float32
BF16 = jnp.bfloat16

D_MODEL = 1024
N_MIXERS = 3
RMS_EPS = 1e-6
LN_EPS = 1e-5
CONV_WIDTH = 3
SGU_CHUNK = 128
SGU_HALF = 2 * D_MODEL
SGU_GROUPS = 8
SGU_GROUP_DIM = SGU_HALF // SGU_GROUPS
RET_HEADS = 4
RET_QK_DIM = D_MODEL // RET_HEADS
RET_V_DIM = 2 * D_MODEL // RET_HEADS
ROPE_BASE = 10000.0
RET_BLOCK = 256

SUBLANES = 8
LANES = 128
BF16_SUBLANES = 16
VMEM_LIMIT_BYTES = 60 * 1024 * 1024

FFN_TILE = 1024
CONV_TILE = 1024
SGU_TILE = 1024
RET_TILE = 1024
FFN_CHUNK = 256
NORM_AHEAD_BLOCKS = 4


def _rmsnorm(x, g):
    ms = jnp.mean(x * x, axis=-1, keepdims=True)
    return x * lax.rsqrt(ms + RMS_EPS) * g


def _dot(a, b):
    return jnp.dot(a, b, preferred_element_type=F32)


def _zero_row_after(value, width):
    bits = pltpu.bitcast(value, jnp.uint32)
    rows, cols = bits.shape
    tile = None
    for r0 in range(0, rows, SUBLANES):
        for c0 in range(0, cols, LANES):
            part = bits[r0:r0 + SUBLANES, c0:c0 + LANES]
            tile = part if tile is None else tile | part
    sixteen = jnp.uint32(16)
    tile = lax.shift_right_logical(lax.shift_right_logical(tile, sixteen), sixteen)
    row = pltpu.bitcast(tile, F32)[0:1, :]
    return jnp.concatenate([row] * (width // LANES), axis=1)


def _start_tile(first_step, x_tile, g_ref, h_ref, hn_ref):
    @pl.when(first_step)
    def _():
        hn_ref[...] = _rmsnorm(x_tile(), g_ref[...]).astype(BF16)

    h_ref[...] = hn_ref[...]


def _norm_block_ahead(xn_rows, g_ref, hn_ref, rows, width):
    hn = _rmsnorm(xn_rows, g_ref[...]).astype(BF16)
    hn_ref[rows, :] = hn
    return _zero_row_after(hn, width)


def _resident(shape):
    zeros = (0,) * len(shape)
    return pl.BlockSpec(shape, lambda *_: zeros, pipeline_mode=pl.Buffered(1))


def _resident_layer(layer, shape):
    index = (layer,) + (0,) * len(shape)
    return pl.BlockSpec((None,) + tuple(shape), lambda *_: index,
                        pipeline_mode=pl.Buffered(1))


def _params(n_axes):
    return pltpu.CompilerParams(
        dimension_semantics=("arbitrary",) * n_axes,
        vmem_limit_bytes=VMEM_LIMIT_BYTES)


def _cast_block_rows(rows, steps):
    for r in range(BF16_SUBLANES, rows + 1, BF16_SUBLANES):
        if rows % r == 0 and rows // r <= steps:
            return r
    raise ValueError(f"no row block for {rows} rows in {steps} steps")


def _call(body, name, grid, in_specs, args, out_spec, out_shape, scratch_shapes, casts):
    steps = math.prod(grid)

    def step_number(*idx):
        n = idx[0]
        for k, extent in zip(idx[1:], grid[1:]):
            n = n * extent + k
        return n

    cast_in, cast_out, cast_shapes = [], [], []
    for w, layer in casts:
        _, rows, cols = w.shape
        block = _cast_block_rows(rows, steps)
        last = rows // block - 1
        cast_in.append(pl.BlockSpec(
            (None, block, cols),
            lambda *idx, layer=layer, last=last: (layer, jnp.minimum(step_number(*idx), last), 0)))
        cast_out.append(pl.BlockSpec(
            (block, cols), lambda *idx, last=last: (jnp.minimum(step_number(*idx), last), 0)))
        cast_shapes.append(jax.ShapeDtypeStruct((rows, cols), BF16))
    n_in, n_cast = len(args), len(casts)

    def kern(*refs):
        in_refs = refs[:n_in]
        src_refs = refs[n_in:n_in + n_cast]
        out_ref = refs[n_in + n_cast]
        dst_refs = refs[n_in + n_cast + 1:n_in + 2 * n_cast + 1]
        scratch = refs[n_in + 2 * n_cast + 1:]
        body(*in_refs, out_ref, *scratch)
        for src, dst in zip(src_refs, dst_refs):
            dst[...] = src[...].astype(BF16)

    outs = pl.pallas_call(
        kern,
        grid=grid,
        in_specs=list(in_specs) + cast_in,
        out_specs=[out_spec] + cast_out,
        out_shape=[out_shape] + cast_shapes,
        scratch_shapes=scratch_shapes,
        compiler_params=_params(len(grid)),
        name=name,
    )(*args, *(w for w, _ in casts))
    return outs[0], list(outs[1:])


def _ffn_kernel(x_ref, xn_ref, g_ref, wg_ref, wu_ref, wd_ref, fg_ref, o_ref,
                h_ref, hn_ref, *, final):
    _start_tile(pl.program_id(0) == 0, lambda: x_ref[...], g_ref, h_ref, hn_ref)
    d_ff = wg_ref.shape[1]
    n_chunks = d_ff // FFN_CHUNK
    ahead = {c: r for r, c in enumerate(range(2, 2 + NORM_AHEAD_BLOCKS))}
    rows_ahead = x_ref.shape[0] // NORM_AHEAD_BLOCKS
    acc = x_ref[...]
    anchor = None
    for c in range(n_chunks):
        f0 = c * FFN_CHUNK
        h = h_ref[...]
        gate = _dot(h, wg_ref[:, f0:f0 + FFN_CHUNK])
        up = _dot(h, wu_ref[:, f0:f0 + FFN_CHUNK])
        act = jax.nn.silu(gate) * up
        if anchor is not None:
            act = act + anchor
            anchor = None
        acc = acc + _dot(act.astype(BF16), wd_ref[f0:f0 + FFN_CHUNK, :])
        if c in ahead:
            rows = slice(ahead[c] * rows_ahead, (ahead[c] + 1) * rows_ahead)
            anchor = _norm_block_ahead(xn_ref[rows, :], g_ref, hn_ref, rows, FFN_CHUNK)
    if final:
        acc = _rmsnorm(acc, fg_ref[...])
    o_ref[...] = acc


def _ffn(x, g, weights, fg, layer, casts, *, final):
    wg, wu, wd = weights
    n, d = x.shape
    d_ff = wg.shape[1]
    assert n % FFN_TILE == 0 and d_ff % FFN_CHUNK == 0
    steps = n // FFN_TILE
    tile = pl.BlockSpec((FFN_TILE, d), lambda i: (i, 0))
    next_tile = pl.BlockSpec((FFN_TILE, d), lambda i: (jnp.minimum(i + 1, steps - 1), 0))
    return _call(
        functools.partial(_ffn_kernel, final=final),
        "ffn_final" if final else "ffn",
        (steps,),
        [tile, next_tile, _resident_layer(layer, (1, d)), _resident((d, d_ff)),
         _resident((d, d_ff)), _resident((d_ff, d)), _resident((1, d))],
        (x, x, g, wg, wu, wd, fg),
        tile, jax.ShapeDtypeStruct((n, d), F32),
        [pltpu.VMEM((FFN_TILE, d), BF16), pltpu.VMEM((FFN_TILE, d), BF16)], casts)


def _conv_kernel(x_ref, g_ref, win_ref, cw_ref, wout_ref, o_ref, cz_ref):
    tm = x_ref.shape[1]
    d = x_ref.shape[2]

    @pl.when(pl.program_id(1) == 0)
    def _():
        cz_ref[0:SUBLANES, :] = jnp.zeros((SUBLANES, d), F32)

    x = x_ref[0]
    h = _rmsnorm(x, g_ref[...]).astype(BF16)
    c_gate = _dot(h, win_ref[:, d:2 * d])
    z = _dot(h, win_ref[:, 2 * d:3 * d])
    cz = c_gate * z
    cz_ref[SUBLANES:SUBLANES + tm, :] = cz
    conv = cz * cw_ref[CONV_WIDTH - 1:CONV_WIDTH, :]
    for k in range(CONV_WIDTH - 1):
        shift = CONV_WIDTH - 1 - k
        conv = conv + cz_ref[SUBLANES - shift:SUBLANES - shift + tm, :] * cw_ref[k:k + 1, :]
    cz_ref[0:SUBLANES, :] = cz_ref[tm:tm + SUBLANES, :]
    b_gate = _dot(h, win_ref[:, 0:d])
    y = (b_gate * conv).astype(BF16)
    o_ref[0] = x + _dot(y, wout_ref[...])


def _conv_mixer(x, g, weights, conv_w, layer, j, casts):
    w_in, w_out = weights
    b, s, d = x.shape
    assert s % CONV_TILE == 0
    tile = pl.BlockSpec((1, CONV_TILE, d), lambda i, t: (i, t, 0))
    return _call(
        _conv_kernel, "conv_mixer", (b, s // CONV_TILE),
        [tile, _resident_layer(layer, (1, d)), _resident((d, 3 * d)),
         _resident_layer(j, (CONV_WIDTH, d)), _resident((d, d))],
        (x, g, w_in, conv_w, w_out),
        tile, jax.ShapeDtypeStruct((b, s, d), F32),
        [pltpu.VMEM((SUBLANES + CONV_TILE, d), F32)], casts)


def _gelu(x):
    return 0.5 * x * (1.0 + lax.erf(x * math.sqrt(0.5)))


def _sgu_kernel(x_ref, g_ref, win_ref, lng_ref, lnb_ref, ws_ref, bs_ref, wout_ref,
                o_ref, v_ref, s_ref):
    tm = x_ref.shape[0]
    x = x_ref[...]
    h = _rmsnorm(x, g_ref[...]).astype(BF16)

    v = _dot(h, win_ref[:, SGU_HALF:2 * SGU_HALF])
    u = _dot(h, win_ref[:, 0:SGU_HALF])
    v = _gelu(v)
    mu = jnp.mean(v, axis=-1, keepdims=True)
    vc = v - mu
    var = jnp.mean(vc * vc, axis=-1, keepdims=True)
    v_ref[...] = (vc * lax.rsqrt(var + LN_EPS) * lng_ref[...] + lnb_ref[...]).astype(BF16)

    row = lax.broadcasted_iota(jnp.int32, (SGU_CHUNK, SGU_CHUNK), 0)
    col = lax.broadcasted_iota(jnp.int32, (SGU_CHUNK, SGU_CHUNK), 1)
    causal = row >= col
    for grp in range(SGU_GROUPS):
        w = jnp.where(causal, ws_ref[grp], 0.0).astype(BF16)
        bias = bs_ref[grp]
        lanes = slice(grp * SGU_GROUP_DIM, (grp + 1) * SGU_GROUP_DIM)
        for t0 in range(0, tm, SGU_CHUNK):
            rows = slice(t0, t0 + SGU_CHUNK)
            s_ref[rows, lanes] = _dot(w, v_ref[rows, lanes]) + bias

    y = (_gelu(u) * s_ref[...]).astype(BF16)
    o_ref[...] = x + _dot(y, wout_ref[...])


def _sgu_mixer(x, g, weights, ln_g, ln_b, w_s, b_s, layer, j, casts):
    w_in, w_out = weights
    n, d = x.shape
    assert n % SGU_TILE == 0 and SGU_TILE % SGU_CHUNK == 0
    tile = pl.BlockSpec((SGU_TILE, d), lambda i: (i, 0))
    return _call(
        _sgu_kernel, "sgu_mixer", (n // SGU_TILE,),
        [tile, _resident_layer(layer, (1, d)), _resident((d, 2 * SGU_HALF)),
         _resident_layer(j, (1, SGU_HALF)), _resident_layer(j, (1, SGU_HALF)),
         _resident_layer(j, (SGU_GROUPS, SGU_CHUNK, SGU_CHUNK)),
         _resident_layer(j, (SGU_GROUPS, SGU_CHUNK, 1)), _resident((SGU_HALF, d))],
        (x, g, w_in, ln_g, ln_b, w_s, b_s, w_out),
        tile, jax.ShapeDtypeStruct((n, d), F32),
        [pltpu.VMEM((SGU_TILE, SGU_HALF), BF16), pltpu.VMEM((SGU_TILE, SGU_HALF), F32)],
        casts)


def _rotary_halves(x1, x2, cos, sin):
    return x1 * cos - x2 * sin, x2 * cos + x1 * sin


def _ret_kernel(x_ref, g_ref, win_ref, wout_ref, cos_ref, sin_ref, o_ref,
                q_ref, k_ref, kd_ref, v_ref, y_ref, state_ref):
    tm = x_ref.shape[1]
    d = x_ref.shape[2]
    half = RET_QK_DIM // 2

    @pl.when(pl.program_id(1) == 0)
    def _():
        state_ref[...] = jnp.zeros(state_ref.shape, F32)

    x = x_ref[0]
    h = _rmsnorm(x, g_ref[...]).astype(BF16)
    cos = cos_ref[...]
    sin = sin_ref[...]

    pos = lax.rem(lax.broadcasted_iota(jnp.int32, (tm, 1), 0), RET_BLOCK).astype(F32)
    row = lax.broadcasted_iota(jnp.int32, (RET_BLOCK, RET_BLOCK), 0)
    col = lax.broadcasted_iota(jnp.int32, (RET_BLOCK, RET_BLOCK), 1)
    diff = (row - col).astype(F32)
    idx = lax.broadcasted_iota(jnp.int32, (RET_BLOCK, 1), 0).astype(F32)

    q = _dot(h, win_ref[:, 0:d])
    k = _dot(h, win_ref[:, d:2 * d])
    for hd in range(RET_HEADS):
        log_gamma = math.log(1.0 - 2.0 ** (-5.0 - hd))
        k_decay = jnp.exp((RET_BLOCK - 1.0 - pos) * log_gamma)
        lo = slice(hd * RET_QK_DIM, hd * RET_QK_DIM + half)
        hi = slice(hd * RET_QK_DIM + half, (hd + 1) * RET_QK_DIM)
        q1, q2 = _rotary_halves(q[:, lo], q[:, hi], cos, sin)
        q_ref[:, lo] = q1.astype(BF16)
        q_ref[:, hi] = q2.astype(BF16)
        k1, k2 = _rotary_halves(k[:, lo], k[:, hi], cos, sin)
        k1 = k1 * (RET_QK_DIM ** -0.5)
        k2 = k2 * (RET_QK_DIM ** -0.5)
        k_ref[:, lo] = k1.astype(BF16)
        k_ref[:, hi] = k2.astype(BF16)
        kd_ref[:, lo] = (k1 * k_decay).astype(BF16)
        kd_ref[:, hi] = (k2 * k_decay).astype(BF16)
    v_ref[...] = _dot(h, win_ref[:, 2 * d:4 * d]).astype(BF16)

    for hd in range(RET_HEADS):
        log_gamma = math.log(1.0 - 2.0 ** (-5.0 - hd))
        d_inner = jnp.where(diff >= 0, jnp.exp(jnp.maximum(diff, 0.0) * log_gamma), 0.0)
        q_decay = jnp.exp((idx + 1.0) * log_gamma)
        block_decay = math.exp(RET_BLOCK * log_gamma)
        qk_lanes = slice(hd * RET_QK_DIM, (hd + 1) * RET_QK_DIM)
        v_lanes = slice(hd * RET_V_DIM, (hd + 1) * RET_V_DIM)
        for t0 in range(0, tm, RET_BLOCK):
            rows = slice(t0, t0 + RET_BLOCK)
            qc = q_ref[rows, qk_lanes]
            vc = v_ref[rows, v_lanes]
            scores = lax.dot_general(qc, k_ref[rows, qk_lanes], (((1,), (1,)), ((), ())),
                                     preferred_element_type=F32) * d_inner
            inner = _dot(scores.astype(BF16), vc)
            state = state_ref[hd]
            cross = _dot(qc, state.astype(BF16)) * q_decay
            state_ref[hd] = state * block_decay + lax.dot_general(
                kd_ref[rows, qk_lanes], vc, (((0,), (0,)), ((), ())),
                preferred_element_type=F32)
            o = inner + cross
            o = o * lax.rsqrt(jnp.mean(o * o, axis=-1, keepdims=True) + RMS_EPS)
            y_ref[rows, v_lanes] = o

    gate = _dot(h, win_ref[:, 4 * d:6 * d])
    y = (jax.nn.silu(gate) * y_ref[...]).astype(BF16)
    o_ref[0] = x + _dot(y, wout_ref[...])


def _ret_mixer(x, g, weights, cos, sin, layer, casts):
    w_in, w_out = weights
    b, s, d = x.shape
    assert s % RET_TILE == 0 and RET_TILE % RET_BLOCK == 0
    tile = pl.BlockSpec((1, RET_TILE, d), lambda i, t: (i, t, 0))
    rope = pl.BlockSpec((RET_TILE, RET_QK_DIM // 2), lambda i, t: (t, 0))
    return _call(
        _ret_kernel, "ret_mixer", (b, s // RET_TILE),
        [tile, _resident_layer(layer, (1, d)), _resident((d, 6 * d)), _resident((2 * d, d)),
         rope, rope],
        (x, g, w_in, w_out, cos, sin),
        tile, jax.ShapeDtypeStruct((b, s, d), F32),
        [pltpu.VMEM((RET_TILE, d), BF16), pltpu.VMEM((RET_TILE, d), BF16),
         pltpu.VMEM((RET_TILE, d), BF16), pltpu.VMEM((RET_TILE, 2 * d), BF16),
         pltpu.VMEM((RET_TILE, 2 * d), F32),
         pltpu.VMEM((RET_HEADS, RET_QK_DIM, RET_V_DIM), F32)],
        casts)


def _rope_tables(seq):
    half = RET_QK_DIM // 2
    pos = jnp.arange(seq, dtype=F32)
    inv_freq = ROPE_BASE ** (-(jnp.arange(half, dtype=F32) / half))
    ang = pos[:, None] * inv_freq[None, :]
    return jnp.cos(ang), jnp.sin(ang)


def kernel(x, norm_mix_g, norm_ffn_g, final_norm_g, conv_w_in, conv_w, conv_w_out,
           sgu_w_in, sgu_ln_g, sgu_ln_b, sgu_w_s, sgu_b_s, sgu_w_out,
           ret_w_in, ret_w_out, ffn_w_gate, ffn_w_up, ffn_w_down):
    bsz, seq, d = x.shape
    depth = norm_mix_g.shape[0]
    cos, sin = _rope_tables(seq)
    rows = lambda v: v[:, None, :]
    norm_mix_g, norm_ffn_g = rows(norm_mix_g), rows(norm_ffn_g)
    final_norm_g = final_norm_g.reshape(1, d)
    sgu_ln_g, sgu_ln_b = rows(sgu_ln_g), rows(sgu_ln_b)
    sgu_b_s = sgu_b_s[..., None]

    mixer_weights = {0: (conv_w_in, conv_w_out), 1: (sgu_w_in, sgu_w_out),
                     2: (ret_w_in, ret_w_out)}
    stages = []
    for i in range(depth):
        j = i // N_MIXERS
        stages.append([(w, j) for w in mixer_weights[i % N_MIXERS]])
        stages.append([(w, i) for w in (ffn_w_gate, ffn_w_up, ffn_w_down)])
    stages.append([])

    weights = [w[layer].astype(BF16) for w, layer in stages[0]]
    for i in range(depth):
        kind = i % N_MIXERS
        j = i // N_MIXERS
        casts = stages[2 * i + 1]
        if kind == 0:
            x, weights = _conv_mixer(x, norm_mix_g, weights, conv_w, i, j, casts)
        elif kind == 1:
            x, weights = _sgu_mixer(x.reshape(bsz * seq, d), norm_mix_g, weights, sgu_ln_g,
                                    sgu_ln_b, sgu_w_s, sgu_b_s, i, j, casts)
            x = x.reshape(bsz, seq, d)
        else:
            x, weights = _ret_mixer(x, norm_mix_g, weights, cos, sin, i, casts)
        x, weights = _ffn(x.reshape(bsz * seq, d), norm_ffn_g, weights, final_norm_g, i,
                          stages[2 * i + 2], final=i == depth - 1)
        x = x.reshape(bsz, seq, d)
    return x
```
